```python
import math
import jax, jax.numpy as jnp
from jax import lax
import numpy as np

D_MODEL = 1024
BATCH = 4
SEQ = 4096
DEPTH = 4

HEAD_DIM = 64
GM_WIDTH = D_MODEL // 4
GM_HEADS = GM_WIDTH // HEAD_DIM
GM_CHUNK = 128
S5_WIDTH = D_MODEL // 4
S5_GROUP_DIM = 16
S5_GROUPS = S5_WIDTH // S5_GROUP_DIM
S5_STATE = 64
S5_DT_MIN = 1e-3
S5_DT_MAX = 1e-1
FOX_WIDTH = D_MODEL // 2
FOX_HEADS = FOX_WIDTH // HEAD_DIM
Q_BLOCK = 128
D_MIX = GM_WIDTH + S5_WIDTH + FOX_WIDTH
D_IN_PROJ = 2 * GM_WIDTH + S5_WIDTH + 3 * FOX_WIDTH + FOX_HEADS
IN_SPLITS = (GM_WIDTH, 2 * GM_WIDTH, 2 * GM_WIDTH + S5_WIDTH,
             2 * GM_WIDTH + S5_WIDTH + FOX_WIDTH,
             2 * GM_WIDTH + S5_WIDTH + 2 * FOX_WIDTH,
             2 * GM_WIDTH + S5_WIDTH + 3 * FOX_WIDTH)
D_FF = ((8 * D_MODEL // 3 + 127) // 128) * 128
CONV_WIDTH = 3
LN_EPS = 1e-5
DN_ALPHA = (2.0 * DEPTH) ** 0.25
DN_BETA = (8.0 * DEPTH) ** -0.25
NEG_INF = -1e30

kernel_name = "hymba_style_gmlp_s5_fox_deepnorm_trunk"


def layer_norm(x, g, b):
    xf = x.astype(jnp.float32)
    mu = jnp.mean(xf, axis=-1, keepdims=True)
    var = jnp.mean(jnp.square(xf - mu), axis=-1, keepdims=True)
    y = (xf - mu) * lax.rsqrt(var + LN_EPS)
    return (y * g.astype(jnp.float32) + b.astype(jnp.float32)).astype(x.dtype)


def gmlp_mixer(u, v, ln_g, ln_b, w_s, b_s):
    B, S, _ = u.shape
    n = S // GM_CHUNK
    v = layer_norm(v.reshape(B, S, GM_HEADS, HEAD_DIM), ln_g, ln_b)
    v = v.reshape(B, n, GM_CHUNK, GM_HEADS, HEAD_DIM)
    causal = jnp.tril(jnp.ones((GM_CHUNK, GM_CHUNK), dtype=bool))
    w = jnp.where(causal[None], w_s, jnp.zeros_like(w_s))
    z = jnp.einsum('hts,bnshd->bnthd', w, v) + b_s.T[None, None, :, :, None]
    out = u.reshape(B, n, GM_CHUNK, GM_HEADS, HEAD_DIM) * z
    return out.reshape(B, S, GM_WIDTH)


def _ssm_combine(left, right):
    a_l, b_l = left
    a_r, b_r = right
    return a_r * a_l, a_r * b_l + b_r


def s5_mixer(u, lam_re, lam_im, log_dt, b_re, b_im, c_re, c_im, d_skip, w_glu, b_glu):
    B, S, _ = u.shape
    f32 = jnp.float32
    uf = u.astype(f32).reshape(B, S, S5_GROUPS, S5_GROUP_DIM)
    lam = lax.complex(lam_re.astype(f32), lam_im.astype(f32))
    dt = jnp.exp(log_dt.astype(f32))[:, None]
    lam_bar = jnp.exp(lam * dt)
    b_mat = lax.complex(b_re.astype(f32), b_im.astype(f32))
    b_bar = ((lam_bar - 1.0) / lam)[:, :, None] * b_mat
    c_mat = lax.complex(c_re.astype(f32), c_im.astype(f32))
    bu = jnp.einsum('gph,bsgh->bsgp', b_bar, uf.astype(jnp.complex64))
    a = jnp.broadcast_to(lam_bar, bu.shape)
    _, states = lax.associative_scan(_ssm_combine, (a, bu), axis=1)
    y = jnp.real(jnp.einsum('ghp,bsgp->bsgh', c_mat, states)) + d_skip.astype(f32) * uf
    y = jax.nn.gelu(y)
    gate = jnp.einsum('gij,bsgj->bsgi', w_glu.astype(f32), y) + b_glu.astype(f32)
    out = y * jax.nn.sigmoid(gate)
    return out.reshape(B, S, S5_WIDTH).astype(u.dtype)


def fox_mixer(q, k, v, f_logit):
    B, S, _ = q.shape
    q = q.reshape(B, S, FOX_HEADS, HEAD_DIM)
    k = k.reshape(B, S, FOX_HEADS, HEAD_DIM)
    v = v.reshape(B, S, FOX_HEADS, HEAD_DIM)
    log_f = jax.nn.log_sigmoid(f_logit.astype(jnp.float32))
    cum = jnp.cumsum(log_f, axis=1).transpose(0, 2, 1)
    scale = HEAD_DIM ** -0.5
    outs = []
    for i in range(S // Q_BLOCK):
        q0, q1 = i * Q_BLOCK, (i + 1) * Q_BLOCK
        logits = jnp.einsum('bthd,bshd->bhts', q[:, q0:q1], k[:, :q1]).astype(jnp.float32) * scale
        decay = cum[:, :, q0:q1, None] - cum[:, :, None, :q1]
        causal = jnp.arange(q1)[None, :] <= (q0 + jnp.arange(Q_BLOCK))[:, None]
        logits = jnp.where(causal, logits + decay, NEG_INF)
        p = jax.nn.softmax(logits, axis=-1)
        outs.append(jnp.einsum('bhts,bshd->bthd', p.astype(v.dtype), v[:, :q1]))
    return jnp.concatenate(outs, axis=1).reshape(B, S, FOX_WIDTH)


def conv_ffn(h, w_up, conv_w, conv_b, w_down):
    S = h.shape[1]
    up = h @ w_up
    a, g = jnp.split(up, 2, axis=-1)
    a_pad = jnp.pad(a, ((0, 0), (CONV_WIDTH - 1, 0), (0, 0)))
    conv = conv_b
    for j in range(CONV_WIDTH):
        conv = conv + conv_w[j] * a_pad[:, j:j + S]
    return (jax.nn.gelu(conv) * g) @ w_down


def setup_inputs(seed: int = 0) -> dict:
    key = jax.random.key(seed)
    ks = jax.random.split(key, 32)
    f32 = jnp.float32
    L, D = DEPTH, D_MODEL
    nrm = lambda k, shape, s: jax.random.normal(k, shape, f32) * s
    n_idx = jnp.arange(S5_STATE, dtype=f32)
    return {
        "x": jax.random.normal(ks[0], (BATCH, SEQ, D), f32),
        "c": jax.random.normal(ks[1], (BATCH, D), f32),
        "w_ada": nrm(ks[2], (L, D, 6 * D), 0.1 * D ** -0.5),
        "b_ada": nrm(ks[3], (L, 6 * D), 0.01),
        "w_in": nrm(ks[4], (L, D, D_IN_PROJ), D ** -0.5),
        "b_f": jax.random.uniform(ks[5], (L, FOX_HEADS), f32, 1.0, 3.0),
        "gm_ln_g": 1.0 + nrm(ks[6], (L, GM_HEADS, HEAD_DIM), 0.02),
        "gm_ln_b": nrm(ks[7], (L, GM_HEADS, HEAD_DIM), 0.02),
        "gm_w_s": nrm(ks[8], (L, GM_HEADS, GM_CHUNK, GM_CHUNK), GM_CHUNK ** -0.5),
        "gm_b_s": 1.0 + nrm(ks[9], (L, GM_HEADS, GM_CHUNK), 0.02),
        "s5_lam_re": -0.5 + nrm(ks[10], (L, S5_GROUPS, S5_STATE), 0.01),
        "s5_lam_im": math.pi * n_idx + nrm(ks[11], (L, S5_GROUPS, S5_STATE), 0.01),
        "s5_log_dt": jax.random.uniform(ks[12], (L, S5_GROUPS), f32,
                                        math.log(S5_DT_MIN), math.log(S5_DT_MAX)),
        "s5_b_re": nrm(ks[13], (L, S5_GROUPS, S5_STATE, S5_GROUP_DIM), (2 * S5_GROUP_DIM) ** -0.5),
        "s5_b_im": nrm(ks[14], (L, S5_GROUPS, S5_STATE, S5_GROUP_DIM), (2 * S5_GROUP_DIM) ** -0.5),
        "s5_c_re": nrm(ks[15], (L, S5_GROUPS, S5_GROUP_DIM, S5_STATE), (2 * S5_STATE) ** -0.5),
        "s5_c_im": nrm(ks[16], (L, S5_GROUPS, S5_GROUP_DIM, S5_STATE), (2 * S5_STATE) ** -0.5),
        "s5_d": nrm(ks[17], (L, S5_GROUPS, S5_GROUP_DIM), 1.0),
        "s5_w_glu": nrm(ks[18], (L, S5_GROUPS, S5_GROUP_DIM, S5_GROUP_DIM), S5_GROUP_DIM ** -0.5),
        "s5_b_glu": nrm(ks[19], (L, S5_GROUPS, S5_GROUP_DIM), 0.02),
        "w_out": nrm(ks[20], (L, D_MIX, D), DN_BETA * D_MIX ** -0.5),
        "ln1_g": 1.0 + nrm(ks[21], (L, D), 0.02),
        "ln1_b": nrm(ks[22], (L, D), 0.02),
        "w_up": nrm(ks[23], (L, D, 2 * D_FF), D ** -0.5),
        "conv_w": nrm(ks[24], (L, CONV_WIDTH, D_FF), CONV_WIDTH ** -0.5),
        "conv_b": nrm(ks[25], (L, D_FF), 0.02),
        "w_down": nrm(ks[26], (L, D_FF, D), DN_BETA * D_FF ** -0.5),
        "ln2_g": 1.0 + nrm(ks[27], (L, D), 0.02),
        "ln2_b": nrm(ks[28], (L, D), 0.02),
    }


def reference(x, c, w_ada, b_ada, w_in, b_f, gm_ln_g, gm_ln_b, gm_w_s, gm_b_s,
              s5_lam_re, s5_lam_im, s5_log_dt, s5_b_re, s5_b_im, s5_c_re, s5_c_im,
              s5_d, s5_w_glu, s5_b_glu, w_out, ln1_g, ln1_b, w_up, conv_w, conv_b,
              w_down, ln2_g, ln2_b):
    cond = jax.nn.silu(c)
    for l in range(DEPTH):
        mod = (cond @ w_ada[l] + b_ada[l])[:, None, :]
        sh1, sc1, g1, sh2, sc2, g2 = jnp.split(mod, 6, axis=-1)

        h = x * (1.0 + sc1) + sh1
        p = h @ w_in[l]
        gm_u, gm_v, s5_in, fq, fk, fv, ff = jnp.split(p, IN_SPLITS, axis=-1)
        y_gm = gmlp_mixer(gm_u, gm_v, gm_ln_g[l], gm_ln_b[l], gm_w_s[l], gm_b_s[l])
        y_s5 = s5_mixer(s5_in, s5_lam_re[l], s5_lam_im[l], s5_log_dt[l], s5_b_re[l], s5_b_im[l],
                        s5_c_re[l], s5_c_im[l], s5_d[l], s5_w_glu[l], s5_b_glu[l])
        y_fox = fox_mixer(fq, fk, fv, ff + b_f[l])
        mix = jnp.concatenate([y_gm, y_s5, y_fox], axis=-1) @ w_out[l]
        x = layer_norm(DN_ALPHA * x + (1.0 + g1) * mix, ln1_g[l], ln1_b[l])

        h = x * (1.0 + sc2) + sh2
        ffn = conv_ffn(h, w_up[l], conv_w[l], conv_b[l], w_down[l])
        x = layer_norm(DN_ALPHA * x + (1.0 + g2) * ffn, ln2_g[l], ln2_b[l])
    return x
```

```python
import functools
import math

import numpy as np
import jax
import jax.numpy as jnp
from jax import lax
from jax.experimental import pallas as pl
from jax.experimental.pallas import tpu as pltpu

F32 = jnp.float32
BF16 = jnp.bfloat16

HEAD_DIM = 64
GM_CHUNK = 128
S5_GROUP_DIM = 16
S5_STATE = 64
CONV_WIDTH = 3
LN_EPS = 1e-5
NEG_INF = -1e30

LANES = 128
SUBLANES = 8
ROW_TILE = 512
S5_CHUNK = 128
S5_NCHUNK = SUBLANES
FOX_BLOCK = 256
VMEM_LIMIT = 56 * 1024 * 1024


def _cparams(n_axes, vmem=VMEM_LIMIT):
    return pltpu.CompilerParams(dimension_semantics=("arbitrary",) * n_axes,
                                vmem_limit_bytes=vmem)


def _layer_norm_rows(y, g, b):
    mu = jnp.mean(y, axis=-1, keepdims=True)
    d = y - mu
    var = jnp.mean(d * d, axis=-1, keepdims=True)
    return d * lax.rsqrt(var + LN_EPS) * g + b


def _gelu_tanh(x):
    c = math.sqrt(2.0 / math.pi)
    return 0.5 * x * (1.0 + jnp.tanh(c * (x + 0.044715 * (x * x * x))))


def _split3(x):
    hi = x.astype(BF16)
    r1 = x - hi.astype(F32)
    mid = r1.astype(BF16)
    lo = (r1 - mid.astype(F32)).astype(BF16)
    return hi, mid, lo


def _adaln_kernel(c_ref, w_ref, b_ref, o_ref):
    c = c_ref[...]
    cond = (c * jax.nn.sigmoid(c)).astype(BF16)
    w = w_ref[0].astype(BF16)
    o_ref[0] = jnp.dot(cond, w, preferred_element_type=F32) + b_ref[0]


def _adaln(c_pad, w_ada, b_ada3):
    depth, d, n = w_ada.shape
    rows = c_pad.shape[0]
    tn = 1536
    return pl.pallas_call(
        _adaln_kernel,
        grid=(depth, n // tn),
        in_specs=[pl.BlockSpec((rows, d), lambda l, j: (0, 0)),
                  pl.BlockSpec((1, d, tn), lambda l, j: (l, 0, j)),
                  pl.BlockSpec((1, 1, tn), lambda l, j: (l, 0, j))],
        out_specs=pl.BlockSpec((1, rows, tn), lambda l, j: (l, 0, j)),
        out_shape=jax.ShapeDtypeStruct((depth, rows, n), F32),
        compiler_params=_cparams(2),
        name="adaln",
    )(c_pad, w_ada, b_ada3)


def _inproj_kernel(x_ref, mod_ref, w_ref, wf_ref,
                   gu_ref, gv_ref, s5_ref, q_ref, k_ref, v_ref, fl_ref, h_scr, *, gm, s5w, fw):
    sh = mod_ref[0, 0:1, :]
    sc = mod_ref[0, 1:2, :]
    h_scr[...] = (x_ref[0] * (1.0 + sc) + sh).astype(BF16)
    h = h_scr[...]

    def proj(lo, hi):
        return jnp.dot(h, w_ref[:, lo:hi], preferred_element_type=F32)

    o = 0
    gu_ref[0] = proj(o, o + gm); o += gm
    gv_ref[0] = proj(o, o + gm); o += gm
    s5_ref[0] = proj(o, o + s5w); o += s5w
    q_ref[0] = (proj(o, o + fw) * (HEAD_DIM ** -0.5)).astype(BF16); o += fw
    k_ref[0] = proj(o, o + fw).astype(BF16); o += fw
    v_ref[0] = proj(o, o + fw).astype(BF16)
    fl_ref[0] = jnp.dot(h, wf_ref[...], preferred_element_type=F32)


def _inproj(x, mod_l, w_main, w_f, gm, s5w, fw):
    b, s, d = x.shape
    tm = ROW_TILE
    row = lambda n: pl.BlockSpec((1, tm, n), lambda i, j: (i, j, 0))
    outs = [jax.ShapeDtypeStruct((b, s, gm), F32), jax.ShapeDtypeStruct((b, s, gm), F32),
            jax.ShapeDtypeStruct((b, s, s5w), F32), jax.ShapeDtypeStruct((b, s, fw), BF16),
            jax.ShapeDtypeStruct((b, s, fw), BF16), jax.ShapeDtypeStruct((b, s, fw), BF16),
            jax.ShapeDtypeStruct((b, s, LANES), F32)]
    return pl.pallas_call(
        functools.partial(_inproj_kernel, gm=gm, s5w=s5w, fw=fw),
        grid=(b, s // tm),
        in_specs=[row(d),
                  pl.BlockSpec((1, 6, d), lambda i, j: (i, 0, 0)),
                  pl.BlockSpec(w_main.shape, lambda i, j: (0, 0)),
                  pl.BlockSpec(w_f.shape, lambda i, j: (0, 0))],
        out_specs=[row(gm), row(gm), row(s5w), row(fw), row(fw), row(fw), row(LANES)],
        out_shape=outs,
        scratch_shapes=[pltpu.VMEM((tm, d), BF16)],
        compiler_params=_cparams(2),
        name="inproj",
    )(x, mod_l, w_main, w_f)


def _gmlp_kernel(u_ref, v_ref, g_ref, b_ref, ws_ref, bs_ref, o_ref, *, heads):
    v = v_ref[0]
    rows, width = v.shape
    lane = lax.broadcasted_iota(jnp.int32, (rows, width), 1)
    head_masks = [(lane >= h * HEAD_DIM) & (lane < (h + 1) * HEAD_DIM) for h in range(heads)]

    def seg_mean(a):
        out = jnp.zeros_like(a)
        for m in head_masks:
            s = jnp.sum(jnp.where(m, a, 0.0), axis=-1, keepdims=True) * (1.0 / HEAD_DIM)
            out = jnp.where(m, s, out)
        return out

    d = v - seg_mean(v)
    var = seg_mean(d * d)
    vn = (d * lax.rsqrt(var + LN_EPS) * g_ref[...] + b_ref[...]).astype(BF16)

    r = lax.broadcasted_iota(jnp.int32, (GM_CHUNK, GM_CHUNK), 0)
    c = lax.broadcasted_iota(jnp.int32, (GM_CHUNK, GM_CHUNK), 1)
    tril = c <= r
    ws = [jnp.where(tril, ws_ref[h], 0.0).astype(BF16) for h in range(heads)]
    lane_c = lax.broadcasted_iota(jnp.int32, (GM_CHUNK, width), 1)
    chunk_masks = [(lane_c >= h * HEAD_DIM) & (lane_c < (h + 1) * HEAD_DIM) for h in range(heads)]
    bs = bs_ref[...]
    for j in range(rows // GM_CHUNK):
        r0 = j * GM_CHUNK
        vc = vn[r0:r0 + GM_CHUNK]
        z = jnp.zeros((GM_CHUNK, width), F32)
        for h in range(heads):
            zh = jnp.dot(ws[h], vc, preferred_element_type=F32)
            z = jnp.where(chunk_masks[h], zh, z)
        o_ref[0, r0:r0 + GM_CHUNK, :] = (u_ref[0, r0:r0 + GM_CHUNK, :] * (z + bs)).astype(BF16)


def _gmlp(gm_u, gm_v, ln_g_row, ln_b_row, w_s, b_s_full):
    b, s, w = gm_u.shape
    heads = w_s.shape[0]
    tm = ROW_TILE
    row = pl.BlockSpec((1, tm, w), lambda i, j: (i, j, 0))
    full = lambda a: pl.BlockSpec(a.shape, lambda i, j: (0,) * a.ndim)
    return pl.pallas_call(
        functools.partial(_gmlp_kernel, heads=heads),
        grid=(b, s // tm),
        in_specs=[row, row, full(ln_g_row), full(ln_b_row), full(w_s), full(b_s_full)],
        out_specs=row,
        out_shape=jax.ShapeDtypeStruct((b, s, w), BF16),
        compiler_params=_cparams(2),
        name="gmlp",
    )(gm_u, gm_v, ln_g_row, ln_b_row, w_s, b_s_full)


def _s5prep_kernel(lre_ref, lim_ref, ldt_ref, bre_ref, bim_ref, cre_ref, cim_ref,
                   lam_ref, laml_ref, bblk_ref, cblk_ref):
    lre = lre_ref[0]
    lim = lim_ref[0]
    dt = jnp.exp(ldt_ref[0])
    mag = jnp.exp(lre * dt)
    lbr = mag * jnp.cos(lim * dt)
    lbi = mag * jnp.sin(lim * dt)
    nr = lbr - 1.0
    den = lre * lre + lim * lim
    cr = (nr * lre + lbi * lim) / den
    ci = (lbi * lre - nr * lim) / den
    n = lre.shape[-1]
    lam_ref[0, :, 0:n] = lbr
    lam_ref[0, :, n:2 * n] = lbi
    pr, pi = lbr, lbi
    for _ in range(int(math.log2(S5_CHUNK))):
        pr, pi = pr * pr - pi * pi, 2.0 * (pr * pi)
    laml_ref[0, :, 0:n] = pr
    laml_ref[0, :, n:2 * n] = pi
    bre = bre_ref[0]
    bim = bim_ref[0]
    bblk_ref[0, :, 0:n] = (cr * bre - ci * bim).astype(BF16)
    bblk_ref[0, :, n:2 * n] = (cr * bim + ci * bre).astype(BF16)
    cblk_ref[0, 0:n, :] = cre_ref[0].astype(BF16)
    cblk_ref[0, n:2 * n, :] = (-cim_ref[0]).astype(BF16)


def _s5prep(lre, lim, ldt, bre_blk, bim_blk, cre_blk, cim_blk):
    depth, _, n = lre.shape
    w = bre_blk.shape[1]
    vec = pl.BlockSpec((1, 1, n), lambda l: (l, 0, 0))
    return pl.pallas_call(
        _s5prep_kernel,
        grid=(depth,),
        in_specs=[vec, vec, vec,
                  pl.BlockSpec((1, w, n), lambda l: (l, 0, 0)),
                  pl.BlockSpec((1, w, n), lambda l: (l, 0, 0)),
                  pl.BlockSpec((1, n, w), lambda l: (l, 0, 0)),
                  pl.BlockSpec((1, n, w), lambda l: (l, 0, 0))],
        out_specs=[pl.BlockSpec((1, 1, 2 * n), lambda l: (l, 0, 0)),
                   pl.BlockSpec((1, 1, 2 * n), lambda l: (l, 0, 0)),
                   pl.BlockSpec((1, w, 2 * n), lambda l: (l, 0, 0)),
                   pl.BlockSpec((1, 2 * n, w), lambda l: (l, 0, 0))],
        out_shape=[jax.ShapeDtypeStruct((depth, 1, 2 * n), F32),
                   jax.ShapeDtypeStruct((depth, 1, 2 * n), F32),
                   jax.ShapeDtypeStruct((depth, w, 2 * n), BF16),
                   jax.ShapeDtypeStruct((depth, 2 * n, w), BF16)],
        compiler_params=_cparams(1),
        name="s5prep",
    )(lre, lim, ldt, bre_blk, bim_blk, cre_blk, cim_blk)


def _s5_kernel(ua_ref, ub_ref, lam_ref, laml_ref, bblk_ref, cblk_ref, d_ref, wglu_ref, bglu_ref,
               o_ref, uperm, r_scr, e_scr, xin_scr, carry, y_scr, *, n):
    nc, lc = S5_NCHUNK, S5_CHUNK
    m_rows = nc * lc

    @pl.when(pl.program_id(1) == 0)
    def _():
        carry[...] = jnp.zeros_like(carry)

    def permute(t, _):
        r0 = pl.multiple_of(t * nc, nc)
        uperm[pl.ds(r0, nc), 0:LANES] = ua_ref[0, pl.ds(t, nc, stride=lc), :]
        uperm[pl.ds(r0, nc), LANES:2 * LANES] = ub_ref[0, pl.ds(t, nc, stride=lc), :]
        return 0
    lax.fori_loop(0, lc, permute, 0)

    up = uperm[...]
    r_scr[...] = jnp.dot(up.astype(BF16), bblk_ref[0], preferred_element_type=F32)

    lr = jnp.broadcast_to(lam_ref[0, :, 0:n], (nc, n))
    li = jnp.broadcast_to(lam_ref[0, :, n:2 * n], (nc, n))

    def scan(x0r, x0i, store):
        def body(t, c):
            xr, xi = c
            r0 = pl.multiple_of(t * nc, nc)
            br = r_scr[pl.ds(r0, nc), 0:n]
            bi = r_scr[pl.ds(r0, nc), n:2 * n]
            nxr = lr * xr - li * xi + br
            nxi = lr * xi + li * xr + bi
            if store:
                r_scr[pl.ds(r0, nc), 0:n] = nxr
                r_scr[pl.ds(r0, nc), n:2 * n] = nxi
            return nxr, nxi
        return lax.fori_loop(0, lc, body, (x0r, x0i), unroll=2)

    zero = jnp.zeros((nc, n), F32)
    er, ei = scan(zero, zero, False)
    e_scr[:, 0:n] = er
    e_scr[:, n:2 * n] = ei

    llr = laml_ref[0, :, 0:n]
    lli = laml_ref[0, :, n:2 * n]
    cur_r = carry[:, 0:n]
    cur_i = carry[:, n:2 * n]
    for c in range(nc):
        xin_scr[c:c + 1, 0:n] = cur_r
        xin_scr[c:c + 1, n:2 * n] = cur_i
        e_r = e_scr[c:c + 1, 0:n]
        e_i = e_scr[c:c + 1, n:2 * n]
        cur_r, cur_i = llr * cur_r - lli * cur_i + e_r, llr * cur_i + lli * cur_r + e_i
    carry[:, 0:n] = cur_r
    carry[:, n:2 * n] = cur_i

    scan(xin_scr[:, 0:n], xin_scr[:, n:2 * n], True)

    step = 256
    for r0 in range(0, m_rows, step):
        xs = r_scr[r0:r0 + step, :].astype(BF16)
        y = jnp.dot(xs, cblk_ref[0], preferred_element_type=F32)
        y = y + d_ref[...] * uperm[r0:r0 + step, :]
        y = _gelu_tanh(y)
        gate = jnp.dot(y.astype(BF16), wglu_ref[...], preferred_element_type=F32) + bglu_ref[...]
        out = y * jax.nn.sigmoid(gate)
        for hf in range(out.shape[1] // LANES):
            y_scr[hf, r0:r0 + step, :] = out[:, hf * LANES:(hf + 1) * LANES]

    for c in range(nc):
        for hf in range(y_scr.shape[0]):
            o_ref[0, c * lc:(c + 1) * lc, hf * LANES:(hf + 1) * LANES] = (
                y_scr[hf, pl.ds(c, lc, stride=nc), :].astype(BF16))


def _s5(u, lam, laml, bblk, cblk, layer, d_row, wglu_blk, bglu_row):
    b, s, w = u.shape
    n = lam.shape[-1] // 2
    m_rows = S5_NCHUNK * S5_CHUNK
    full = lambda a: pl.BlockSpec(a.shape, lambda i, j: (0,) * a.ndim)
    per_layer = lambda a: pl.BlockSpec((1,) + a.shape[1:], lambda i, j: (layer,) + (0,) * (a.ndim - 1))
    return pl.pallas_call(
        functools.partial(_s5_kernel, n=n),
        grid=(b, s // m_rows),
        in_specs=[pl.BlockSpec((1, m_rows, LANES), lambda i, j: (i, j, 0)),
                  pl.BlockSpec((1, m_rows, LANES), lambda i, j: (i, j, 1)),
                  per_layer(lam), per_layer(laml), per_layer(bblk), per_layer(cblk),
                  full(d_row), full(wglu_blk), full(bglu_row)],
        out_specs=pl.BlockSpec((1, m_rows, w), lambda i, j: (i, j, 0)),
        out_shape=jax.ShapeDtypeStruct((b, s, w), BF16),
        scratch_shapes=[pltpu.VMEM((m_rows, w), F32),
                        pltpu.VMEM((m_rows, 2 * n), F32),
                        pltpu.VMEM((S5_NCHUNK, 2 * n), F32),
                        pltpu.VMEM((S5_NCHUNK, 2 * n), F32),
                        pltpu.VMEM((1, 2 * n), F32),
                        pltpu.VMEM((w // LANES, m_rows, LANES), F32)],
        compiler_params=_cparams(2),
        name="s5",
    )(u, u, lam, laml, bblk, cblk, d_row, wglu_blk, bglu_row)


def _fox_bias_tables(heads):
    sel = np.zeros((LANES, 2 * heads * LANES), np.float32)
    ones = np.zeros((1, 2 * heads * LANES), np.float32)
    koff = heads * LANES
    for h in range(heads):
        base = h * LANES + (HEAD_DIM if h % 2 == 0 else 0)
        for part in range(3):
            sel[part * heads + h, base + part] = 1.0
            ones[0, base + 3 + part] = 1.0
            ones[0, koff + base + part] = 1.0
            sel[part * heads + h, koff + base + 3 + part] = -1.0
    return jnp.asarray(sel, BF16), jnp.asarray(ones, F32)


def _foxprep_kernel(fl_ref, q_ref, k_ref, bf_ref, sel_ref, ones_ref, qa_ref, ka_ref, carry, cum_scr,
                    *, heads):
    @pl.when(pl.program_id(1) == 0)
    def _():
        carry[...] = jnp.zeros_like(carry)

    rows = fl_ref.shape[1]
    x = fl_ref[0] + bf_ref[...]
    lane = lax.broadcasted_iota(jnp.int32, (rows, LANES), 1)
    ls = jnp.minimum(x, 0.0) - jnp.log1p(jnp.exp(-jnp.abs(x)))
    ls = jnp.where(lane < heads, ls, 0.0)

    r = lax.broadcasted_iota(jnp.int32, (LANES, LANES), 0)
    c = lax.broadcasted_iota(jnp.int32, (LANES, LANES), 1)
    ltri = jnp.where(c <= r, 1.0, 0.0).astype(BF16)
    run = carry[...]
    for j in range(rows // LANES):
        blk = ls[j * LANES:(j + 1) * LANES]
        hi, mid, lo = _split3(blk)
        loc = (jnp.dot(ltri, hi, preferred_element_type=F32)
               + jnp.dot(ltri, mid, preferred_element_type=F32)
               + jnp.dot(ltri, lo, preferred_element_type=F32))
        cum_blk = loc + run
        run = cum_blk[LANES - 1:LANES, :]
        cum_scr[j * LANES:(j + 1) * LANES, :] = cum_blk
    carry[...] = run

    chi, cmid, clo = _split3(cum_scr[...])
    c24 = (chi.astype(F32) + pltpu.roll(cmid.astype(F32), heads, 1)
           + pltpu.roll(clo.astype(F32), 2 * heads, 1)).astype(BF16)
    bias = jnp.dot(c24, sel_ref[...], preferred_element_type=F32) + ones_ref[...]
    bias = bias.astype(BF16)
    koff = heads * LANES
    for h in range(heads):
        pair = (h // 2) * LANES
        keep = (lane < HEAD_DIM) if h % 2 == 0 else (lane >= HEAD_DIM)
        qa_ref[0, h] = jnp.where(keep, q_ref[0, :, pair:pair + LANES],
                                 bias[:, h * LANES:(h + 1) * LANES])
        ka_ref[0, h] = jnp.where(keep, k_ref[0, :, pair:pair + LANES],
                                 bias[:, koff + h * LANES:koff + (h + 1) * LANES])


def _foxprep(fl, q, k, bf_row, heads):
    b, s, fw = q.shape
    tm = ROW_TILE
    sel, ones = _fox_bias_tables(heads)
    row = lambda n: pl.BlockSpec((1, tm, n), lambda i, j: (i, j, 0))
    full = lambda a: pl.BlockSpec(a.shape, lambda i, j: (0,) * a.ndim)
    aug = pl.BlockSpec((1, heads, tm, LANES), lambda i, j: (i, 0, j, 0))
    return pl.pallas_call(
        functools.partial(_foxprep_kernel, heads=heads),
        grid=(b, s // tm),
        in_specs=[row(LANES), row(fw), row(fw), full(bf_row), full(sel), full(ones)],
        out_specs=[aug, aug],
        out_shape=[jax.ShapeDtypeStruct((b, heads, s, LANES), BF16)] * 2,
        scratch_shapes=[pltpu.VMEM((1, LANES), F32), pltpu.VMEM((tm, LANES), F32)],
        compiler_params=_cparams(2),
        name="foxprep",
    )(fl, q, k, bf_row, sel, ones)


def _fox_kernel(qa_ref, ka_ref, v_ref, o_ref):
    qi = pl.program_id(2)
    tq = qa_ref.shape[2]
    tk = tq
    nt = (((1,), (1,)), ((), ()))
    row = lax.broadcasted_iota(jnp.int32, (tq, tk), 0)
    col = lax.broadcasted_iota(jnp.int32, (tq, tk), 1)
    causal = col <= row
    outs = []
    for hh in range(2):
        q = qa_ref[0, hh]

        def block(kj, state, masked):
            m, l, acc = state
            k0 = pl.multiple_of(kj * tk, tk)
            kb = ka_ref[0, hh, pl.ds(k0, tk), :]
            s = lax.dot_general(q, kb, nt, preferred_element_type=F32)
            if masked:
                s = jnp.where(causal, s, NEG_INF)
            m_new = jnp.maximum(m, jnp.max(s, axis=-1, keepdims=True))
            alpha = jnp.exp(m - m_new)
            p = jnp.exp(s - m_new)
            l_new = alpha * l + jnp.sum(p, axis=-1, keepdims=True)
            pv = jnp.dot(p.astype(BF16), v_ref[0, pl.ds(k0, tk), :], preferred_element_type=F32)
            return m_new, l_new, alpha * acc + pv

        init = (jnp.full((tq, 1), -jnp.inf, F32), jnp.zeros((tq, 1), F32),
                jnp.zeros((tq, LANES), F32))
        state = lax.fori_loop(0, qi, lambda kj, st: block(kj, st, False), init)
        m, l, acc = block(qi, state, True)
        outs.append(acc / l)
    lane = lax.broadcasted_iota(jnp.int32, (tq, LANES), 1)
    o_ref[0] = jnp.where(lane < HEAD_DIM, outs[0], outs[1]).astype(BF16)


def _fox(q_aug, k_aug, v):
    b, heads, s, _ = q_aug.shape
    fw = v.shape[-1]
    tq = FOX_BLOCK
    return pl.pallas_call(
        _fox_kernel,
        grid=(b, heads // 2, s // tq),
        in_specs=[pl.BlockSpec((1, 2, tq, LANES), lambda i, p, j: (i, p, j, 0)),
                  pl.BlockSpec((1, 2, s, LANES), lambda i, p, j: (i, p, 0, 0)),
                  pl.BlockSpec((1, s, LANES), lambda i, p, j: (i, 0, p))],
        out_specs=pl.BlockSpec((1, tq, LANES), lambda i, p, j: (i, j, p)),
        out_shape=jax.ShapeDtypeStruct((b, s, fw), BF16),
        compiler_params=_cparams(3),
        name="fox",
    )(q_aug, k_aug, v)


def _outproj_kernel(ygm_ref, ys5_ref, yfox_ref, w_ref, x_ref, mod_ref, g_ref, b_ref, o_ref,
                    *, alpha, gm, s5w):
    mix = jnp.dot(ygm_ref[0], w_ref[0:gm, :], preferred_element_type=F32)
    mix += jnp.dot(ys5_ref[0], w_ref[gm:gm + s5w, :], preferred_element_type=F32)
    mix += jnp.dot(yfox_ref[0], w_ref[gm + s5w:, :], preferred_element_type=F32)
    gate = mod_ref[0, 2:3, :]
    y = alpha * x_ref[0] + (1.0 + gate) * mix
    o_ref[0] = _layer_norm_rows(y, g_ref[...], b_ref[...])


def _outproj(y_gm, y_s5, y_fox, w_out, x, mod_l, g_row, b_row, alpha):
    b, s, d = x.shape
    gm, s5w, fw = y_gm.shape[-1], y_s5.shape[-1], y_fox.shape[-1]
    tm = ROW_TILE
    row = lambda n: pl.BlockSpec((1, tm, n), lambda i, j: (i, j, 0))
    full = lambda a: pl.BlockSpec(a.shape, lambda i, j: (0,) * a.ndim)
    return pl.pallas_call(
        functools.partial(_outproj_kernel, alpha=alpha, gm=gm, s5w=s5w),
        grid=(b, s // tm),
        in_specs=[row(gm), row(s5w), row(fw), full(w_out), row(d),
                  pl.BlockSpec((1, 6, d), lambda i, j: (i, 0, 0)), full(g_row), full(b_row)],
        out_specs=row(d),
        out_shape=jax.ShapeDtypeStruct((b, s, d), F32),
        compiler_params=_cparams(2),
        name="outproj",
    )(y_gm, y_s5, y_fox, w_out, x, mod_l, g_row, b_row)


def _ffn_kernel(x_ref, mod_ref, wup_ref, cw_ref, cb_ref, wdn_ref, g_ref, b_ref, o_ref,
                h_scr, halo, acc, *, alpha, dff, chunk):
    @pl.when(pl.program_id(1) == 0)
    def _():
        halo[...] = jnp.zeros_like(halo)

    x = x_ref[0]
    rows = x.shape[0]
    sh = mod_ref[0, 3:4, :]
    sc = mod_ref[0, 4:5, :]
    gate = mod_ref[0, 5:6, :]
    h_scr[...] = (x * (1.0 + sc) + sh).astype(BF16)
    h = h_scr[...]
    rid = lax.broadcasted_iota(jnp.int32, (rows, chunk), 0)
    for ci in range(dff // chunk):
        c0 = ci * chunk
        a = jnp.dot(h, wup_ref[:, c0:c0 + chunk], preferred_element_type=F32)
        gt = jnp.dot(h, wup_ref[:, dff + c0:dff + c0 + chunk], preferred_element_type=F32)
        p1 = halo[SUBLANES - 1:SUBLANES, c0:c0 + chunk]
        p2 = halo[SUBLANES - 2:SUBLANES - 1, c0:c0 + chunk]
        a1 = jnp.where(rid == 0, p1, pltpu.roll(a, 1, 0))
        a2 = jnp.where(rid == 0, p2, jnp.where(rid == 1, p1, pltpu.roll(a, 2, 0)))
        halo[:, c0:c0 + chunk] = a[rows - SUBLANES:rows, :]
        conv = (cb_ref[:, c0:c0 + chunk] + cw_ref[0:1, c0:c0 + chunk] * a2
                + cw_ref[1:2, c0:c0 + chunk] * a1 + cw_ref[2:3, c0:c0 + chunk] * a)
        act = (_gelu_tanh(conv) * gt).astype(BF16)
        part = jnp.dot(act, wdn_ref[c0:c0 + chunk, :], preferred_element_type=F32)
        if ci == 0:
            acc[...] = part
        else:
            acc[...] += part
    y = alpha * x + (1.0 + gate) * acc[...]
    o_ref[0] = _layer_norm_rows(y, g_ref[...], b_ref[...])


def _ffn(x, mod_l, w_up, conv_w, conv_b_row, w_down, g_row, b_row, alpha):
    b, s, d = x.shape
    dff = w_down.shape[0]
    tm = ROW_TILE
    chunk = 256
    row = pl.BlockSpec((1, tm, d), lambda i, j: (i, j, 0))
    full = lambda a: pl.BlockSpec(a.shape, lambda i, j: (0,) * a.ndim)
    resident = lambda a: pl.BlockSpec(a.shape, lambda i, j: (0,) * a.ndim,
                                      pipeline_mode=pl.Buffered(1))
    return pl.pallas_call(
        functools.partial(_ffn_kernel, alpha=alpha, dff=dff, chunk=chunk),
        grid=(b, s // tm),
        in_specs=[row, pl.BlockSpec((1, 6, d), lambda i, j: (i, 0, 0)),
                  resident(w_up), full(conv_w), full(conv_b_row), resident(w_down),
                  full(g_row), full(b_row)],
        out_specs=row,
        out_shape=jax.ShapeDtypeStruct((b, s, d), F32),
        scratch_shapes=[pltpu.VMEM((tm, d), BF16),
                        pltpu.VMEM((SUBLANES, dff), F32),
                        pltpu.VMEM((tm, d), F32)],
        compiler_params=_cparams(2),
        name="ffn",
    )(x, mod_l, w_up, conv_w, conv_b_row, w_down, g_row, b_row)


def _block_diag(a):
    depth, g, r, c = a.shape
    eye = jnp.eye(g, dtype=a.dtype)
    return (a[:, :, :, None, :] * eye[None, :, None, :, None]).reshape(depth, g * r, g * c)


def kernel(x, c, w_ada, b_ada, w_in, b_f, gm_ln_g, gm_ln_b, gm_w_s, gm_b_s, s5_lam_re, s5_lam_im, s5_log_dt, s5_b_re, s5_b_im, s5_c_re, s5_c_im, s5_d, s5_w_glu, s5_b_glu, w_out, ln1_g, ln1_b, w_up, conv_w, conv_b, w_down, ln2_g, ln2_b):
    depth, d, _ = w_in.shape
    batch = x.shape[0]
    gm_heads = gm_w_s.shape[1]
    gm = gm_heads * HEAD_DIM
    groups = s5_lam_re.shape[1]
    s5w = groups * S5_GROUP_DIM
    fox_heads = b_f.shape[1]
    fw = fox_heads * HEAD_DIM
    n_main = 2 * gm + s5w + 3 * fw
    alpha = (2.0 * depth) ** 0.25

    pad_rows = 2 * SUBLANES
    c_pad = jnp.pad(c, ((0, pad_rows - batch), (0, 0)))
    mod = _adaln(c_pad, w_ada, b_ada[:, None, :])[:, :batch]
    mod = mod.reshape(depth, batch, 6, d)

    n_state = groups * S5_STATE
    lre = s5_lam_re.reshape(depth, 1, n_state)
    lim = s5_lam_im.reshape(depth, 1, n_state)
    ldt = jnp.repeat(s5_log_dt, S5_STATE, axis=-1).reshape(depth, 1, n_state)
    bre_blk = _block_diag(jnp.swapaxes(s5_b_re, -1, -2))
    bim_blk = _block_diag(jnp.swapaxes(s5_b_im, -1, -2))
    cre_blk = _block_diag(jnp.swapaxes(s5_c_re, -1, -2))
    cim_blk = _block_diag(jnp.swapaxes(s5_c_im, -1, -2))
    lam, laml, bblk, cblk = _s5prep(lre, lim, ldt, bre_blk, bim_blk, cre_blk, cim_blk)
    wglu_blk = _block_diag(jnp.swapaxes(s5_w_glu, -1, -2)).astype(BF16)

    w_in_bf = w_in.astype(BF16)
    w_out_bf = w_out.astype(BF16)
    w_up_bf = w_up.astype(BF16)
    w_down_bf = w_down.astype(BF16)

    for l in range(depth):
        mod_l = mod[l]
        w_main = w_in_bf[l, :, :n_main]
        w_f = jnp.pad(w_in_bf[l, :, n_main:], ((0, 0), (0, LANES - fox_heads)))
        gm_u, gm_v, s5_in, q, k, v, fl = _inproj(x, mod_l, w_main, w_f, gm, s5w, fw)

        bs_full = jnp.repeat(gm_b_s[l].T, HEAD_DIM, axis=1)
        y_gm = _gmlp(gm_u, gm_v, gm_ln_g[l].reshape(1, gm), gm_ln_b[l].reshape(1, gm),
                     gm_w_s[l], bs_full)

        y_s5 = _s5(s5_in, lam, laml, bblk, cblk, l, s5_d[l].reshape(1, s5w), wglu_blk[l],
                   s5_b_glu[l].reshape(1, s5w))

        bf_row = jnp.pad(b_f[l][None, :], ((0, 0), (0, LANES - fox_heads)))
        q_aug, k_aug = _foxprep(fl, q, k, bf_row, fox_heads)
        y_fox = _fox(q_aug, k_aug, v)

        x = _outproj(y_gm, y_s5, y_fox, w_out_bf[l], x, mod_l, ln1_g[l][None, :], ln1_b[l][None, :],
                     alpha)
        x = _ffn(x, mod_l, w_up_bf[l], conv_w[l], conv_b[l][None, :], w_down_bf[l],
                 ln2_g[l][None, :], ln2_b[l][None, :], alpha)
    return x
```

```python
import functools
import math

import numpy as np
import jax
import jax.numpy as jnp
from jax import lax
from jax.experimental import pallas as pl
from jax.experimental.pallas import tpu as pltpu

F32 = jnp.float32
BF16 = jnp.bfloat16

HEAD_DIM = 64
GM_CHUNK = 128
S5_GROUP_DIM = 16
S5_STATE = 64
CONV_WIDTH = 3
LN_EPS = 1e-5
NEG_INF = -1e30
LOG2E = math.log2(math.e)

LANES = 128
SUBLANES = 8
ROW_TILE = 512
S5_CHUNK = 128
S5_NCHUNK = SUBLANES
FOX_BLOCK = 512
VMEM_LIMIT = 56 * 1024 * 1024


def _cparams(n_axes, vmem=VMEM_LIMIT):
    return pltpu.CompilerParams(dimension_semantics=("arbitrary",) * n_axes,
                                vmem_limit_bytes=vmem)


def _layer_norm_rows(y, g, b):
    mu = jnp.mean(y, axis=-1, keepdims=True)
    d = y - mu
    var = jnp.mean(d * d, axis=-1, keepdims=True)
    return d * lax.rsqrt(var + LN_EPS) * g + b


def _gelu_tanh(x):
    c = math.sqrt(2.0 / math.pi)
    return 0.5 * x * (1.0 + jnp.tanh(c * (x + 0.044715 * (x * x * x))))


def _split3(x):
    hi = x.astype(BF16)
    r1 = x - hi.astype(F32)
    mid = r1.astype(BF16)
    lo = (r1 - mid.astype(F32)).astype(BF16)
    return hi, mid, lo


def _adaln_kernel(c_ref, w_ref, b_ref, o_ref):
    c = c_ref[...]
    cond = (c * jax.nn.sigmoid(c)).astype(BF16)
    w = w_ref[0].astype(BF16)
    o_ref[0] = jnp.dot(cond, w, preferred_element_type=F32) + b_ref[0]


def _adaln(c_pad, w_ada, b_ada3):
    depth, d, n = w_ada.shape
    rows = c_pad.shape[0]
    tn = 1536
    return pl.pallas_call(
        _adaln_kernel,
        grid=(depth, n // tn),
        in_specs=[pl.BlockSpec((rows, d), lambda l, j: (0, 0)),
                  pl.BlockSpec((1, d, tn), lambda l, j: (l, 0, j)),
                  pl.BlockSpec((1, 1, tn), lambda l, j: (l, 0, j))],
        out_specs=pl.BlockSpec((1, rows, tn), lambda l, j: (l, 0, j)),
        out_shape=jax.ShapeDtypeStruct((depth, rows, n), F32),
        compiler_params=_cparams(2),
        name="adaln",
    )(c_pad, w_ada, b_ada3)


def _inproj_kernel(x_ref, mod_ref, w_ref, wf_ref,
                   gu_ref, gv_ref, s5_ref, q_ref, k_ref, v_ref, fl_ref, h_scr, *, gm, s5w, fw):
    sh = mod_ref[0, 0:1, :]
    sc = mod_ref[0, 1:2, :]
    h_scr[...] = (x_ref[0] * (1.0 + sc) + sh).astype(BF16)
    h = h_scr[...]

    def proj(lo, hi):
        return jnp.dot(h, w_ref[:, lo:hi], preferred_element_type=F32)

    o = 0
    gu_ref[0] = proj(o, o + gm); o += gm
    gv_ref[0] = proj(o, o + gm); o += gm
    s5_ref[0] = proj(o, o + s5w); o += s5w
    q_ref[0] = (proj(o, o + fw) * (LOG2E * HEAD_DIM ** -0.5)).astype(BF16); o += fw
    k_ref[0] = proj(o, o + fw).astype(BF16); o += fw
    v_ref[0] = proj(o, o + fw).astype(BF16)
    fl_ref[0] = jnp.dot(h, wf_ref[...], preferred_element_type=F32)


def _inproj(x, mod_l, w_main, w_f, gm, s5w, fw):
    b, s, d = x.shape
    tm = ROW_TILE
    row = lambda n: pl.BlockSpec((1, tm, n), lambda i, j: (i, j, 0))
    outs = [jax.ShapeDtypeStruct((b, s, gm), F32), jax.ShapeDtypeStruct((b, s, gm), F32),
            jax.ShapeDtypeStruct((b, s, s5w), F32), jax.ShapeDtypeStruct((b, s, fw), BF16),
            jax.ShapeDtypeStruct((b, s, fw), BF16), jax.ShapeDtypeStruct((b, s, fw), BF16),
            jax.ShapeDtypeStruct((b, s, LANES), F32)]
    return pl.pallas_call(
        functools.partial(_inproj_kernel, gm=gm, s5w=s5w, fw=fw),
        grid=(b, s // tm),
        in_specs=[row(d),
                  pl.BlockSpec((1, 6, d), lambda i, j: (i, 0, 0)),
                  pl.BlockSpec(w_main.shape, lambda i, j: (0, 0)),
                  pl.BlockSpec(w_f.shape, lambda i, j: (0, 0))],
        out_specs=[row(gm), row(gm), row(s5w), row(fw), row(fw), row(fw), row(LANES)],
        out_shape=outs,
        scratch_shapes=[pltpu.VMEM((tm, d), BF16)],
        compiler_params=_cparams(2),
        name="inproj",
    )(x, mod_l, w_main, w_f)


def _gmlp_kernel(u_ref, v_ref, g_ref, b_ref, ws_ref, bs_ref, o_ref, *, heads):
    v = v_ref[0]
    rows, width = v.shape
    lane = lax.broadcasted_iota(jnp.int32, (rows, width), 1)
    head_masks = [(lane >= h * HEAD_DIM) & (lane < (h + 1) * HEAD_DIM) for h in range(heads)]

    def seg_mean(a):
        out = jnp.zeros_like(a)
        for m in head_masks:
            s = jnp.sum(jnp.where(m, a, 0.0), axis=-1, keepdims=True) * (1.0 / HEAD_DIM)
            out = jnp.where(m, s, out)
        return out

    d = v - seg_mean(v)
    var = seg_mean(d * d)
    vn = (d * lax.rsqrt(var + LN_EPS) * g_ref[...] + b_ref[...]).astype(BF16)

    r = lax.broadcasted_iota(jnp.int32, (GM_CHUNK, GM_CHUNK), 0)
    c = lax.broadcasted_iota(jnp.int32, (GM_CHUNK, GM_CHUNK), 1)
    tril = c <= r
    ws = [jnp.where(tril, ws_ref[h], 0.0).astype(BF16) for h in range(heads)]
    lane_c = lax.broadcasted_iota(jnp.int32, (GM_CHUNK, width), 1)
    chunk_masks = [(lane_c >= h * HEAD_DIM) & (lane_c < (h + 1) * HEAD_DIM) for h in range(heads)]
    bs = bs_ref[...]
    for j in range(rows // GM_CHUNK):
        r0 = j * GM_CHUNK
        vc = vn[r0:r0 + GM_CHUNK]
        z = jnp.zeros((GM_CHUNK, width), F32)
        for h in range(heads):
            zh = jnp.dot(ws[h], vc, preferred_element_type=F32)
            z = jnp.where(chunk_masks[h], zh, z)
        o_ref[0, r0:r0 + GM_CHUNK, :] = (u_ref[0, r0:r0 + GM_CHUNK, :] * (z + bs)).astype(BF16)


def _gmlp(gm_u, gm_v, ln_g_row, ln_b_row, w_s, b_s_full):
    b, s, w = gm_u.shape
    heads = w_s.shape[0]
    tm = ROW_TILE
    row = pl.BlockSpec((1, tm, w), lambda i, j: (i, j, 0))
    full = lambda a: pl.BlockSpec(a.shape, lambda i, j: (0,) * a.ndim)
    return pl.pallas_call(
        functools.partial(_gmlp_kernel, heads=heads),
        grid=(b, s // tm),
        in_specs=[row, row, full(ln_g_row), full(ln_b_row), full(w_s), full(b_s_full)],
        out_specs=row,
        out_shape=jax.ShapeDtypeStruct((b, s, w), BF16),
        compiler_params=_cparams(2),
        name="gmlp",
    )(gm_u, gm_v, ln_g_row, ln_b_row, w_s, b_s_full)


def _s5prep_kernel(lre_ref, lim_ref, ldt_ref, bre_ref, bim_ref, cre_ref, cim_ref,
                   lam_ref, laml_ref, bblk_ref, cblk_ref):
    lre = lre_ref[0]
    lim = lim_ref[0]
    dt = jnp.exp(ldt_ref[0])
    mag = jnp.exp(lre * dt)
    lbr = mag * jnp.cos(lim * dt)
    lbi = mag * jnp.sin(lim * dt)
    nr = lbr - 1.0
    den = lre * lre + lim * lim
    cr = (nr * lre + lbi * lim) / den
    ci = (lbi * lre - nr * lim) / den
    n = lre.shape[-1]
    lam_ref[0, :, 0:n] = lbr
    lam_ref[0, :, n:2 * n] = lbi
    pr, pi = lbr, lbi
    for _ in range(int(math.log2(S5_CHUNK))):
        pr, pi = pr * pr - pi * pi, 2.0 * (pr * pi)
    laml_ref[0, :, 0:n] = pr
    laml_ref[0, :, n:2 * n] = pi
    bre = bre_ref[0]
    bim = bim_ref[0]
    bblk_ref[0, :, 0:n] = (cr * bre - ci * bim).astype(BF16)
    bblk_ref[0, :, n:2 * n] = (cr * bim + ci * bre).astype(BF16)
    cblk_ref[0, 0:n, :] = cre_ref[0].astype(BF16)
    cblk_ref[0, n:2 * n, :] = (-cim_ref[0]).astype(BF16)


def _s5prep(lre, lim, ldt, bre_blk, bim_blk, cre_blk, cim_blk):
    depth, _, n = lre.shape
    w = bre_blk.shape[1]
    vec = pl.BlockSpec((1, 1, n), lambda l: (l, 0, 0))
    return pl.pallas_call(
        _s5prep_kernel,
        grid=(depth,),
        in_specs=[vec, vec, vec,
                  pl.BlockSpec((1, w, n), lambda l: (l, 0, 0)),
                  pl.BlockSpec((1, w, n), lambda l: (l, 0, 0)),
                  pl.BlockSpec((1, n, w), lambda l: (l, 0, 0)),
                  pl.BlockSpec((1, n, w), lambda l: (l, 0, 0))],
        out_specs=[pl.BlockSpec((1, 1, 2 * n), lambda l: (l, 0, 0)),
                   pl.BlockSpec((1, 1, 2 * n), lambda l: (l, 0, 0)),
                   pl.BlockSpec((1, w, 2 * n), lambda l: (l, 0, 0)),
                   pl.BlockSpec((1, 2 * n, w), lambda l: (l, 0, 0))],
        out_shape=[jax.ShapeDtypeStruct((depth, 1, 2 * n), F32),
                   jax.ShapeDtypeStruct((depth, 1, 2 * n), F32),
                   jax.ShapeDtypeStruct((depth, w, 2 * n), BF16),
                   jax.ShapeDtypeStruct((depth, 2 * n, w), BF16)],
        compiler_params=_cparams(1),
        name="s5prep",
    )(lre, lim, ldt, bre_blk, bim_blk, cre_blk, cim_blk)


def _s5_kernel(ua_ref, ub_ref, lam_ref, laml_ref, bblk_ref, cblk_ref, d_ref, wglu_ref, bglu_ref,
               o_ref, uperm, r_scr, e_scr, xin_scr, carry, y_scr, *, n):
    nc, lc = S5_NCHUNK, S5_CHUNK
    m_rows = nc * lc

    @pl.when(pl.program_id(1) == 0)
    def _():
        carry[...] = jnp.zeros_like(carry)

    def permute(t, _):
        r0 = pl.multiple_of(t * nc, nc)
        uperm[pl.ds(r0, nc), 0:LANES] = ua_ref[0, pl.ds(t, nc, stride=lc), :]
        uperm[pl.ds(r0, nc), LANES:2 * LANES] = ub_ref[0, pl.ds(t, nc, stride=lc), :]
        return 0
    lax.fori_loop(0, lc, permute, 0)

    up = uperm[...]
    r_scr[...] = jnp.dot(up.astype(BF16), bblk_ref[0], preferred_element_type=F32)

    lr = jnp.broadcast_to(lam_ref[0, :, 0:n], (nc, n))
    li = jnp.broadcast_to(lam_ref[0, :, n:2 * n], (nc, n))

    def scan(x0r, x0i, store):
        def body(t, c):
            xr, xi = c
            r0 = pl.multiple_of(t * nc, nc)
            br = r_scr[pl.ds(r0, nc), 0:n]
            bi = r_scr[pl.ds(r0, nc), n:2 * n]
            nxr = lr * xr - li * xi + br
            nxi = lr * xi + li * xr + bi
            if store:
                r_scr[pl.ds(r0, nc), 0:n] = nxr
                r_scr[pl.ds(r0, nc), n:2 * n] = nxi
            return nxr, nxi
        return lax.fori_loop(0, lc, body, (x0r, x0i), unroll=2)

    zero = jnp.zeros((nc, n), F32)
    er, ei = scan(zero, zero, False)
    e_scr[:, 0:n] = er
    e_scr[:, n:2 * n] = ei

    llr = laml_ref[0, :, 0:n]
    lli = laml_ref[0, :, n:2 * n]
    cur_r = carry[:, 0:n]
    cur_i = carry[:, n:2 * n]
    for c in range(nc):
        xin_scr[c:c + 1, 0:n] = cur_r
        xin_scr[c:c + 1, n:2 * n] = cur_i
        e_r = e_scr[c:c + 1, 0:n]
        e_i = e_scr[c:c + 1, n:2 * n]
        cur_r, cur_i = llr * cur_r - lli * cur_i + e_r, llr * cur_i + lli * cur_r + e_i
    carry[:, 0:n] = cur_r
    carry[:, n:2 * n] = cur_i

    scan(xin_scr[:, 0:n], xin_scr[:, n:2 * n], True)

    step = 256
    for r0 in range(0, m_rows, step):
        xs = r_scr[r0:r0 + step, :].astype(BF16)
        y = jnp.dot(xs, cblk_ref[0], preferred_element_type=F32)
        y = y + d_ref[...] * uperm[r0:r0 + step, :]
        y = _gelu_tanh(y)
        gate = jnp.dot(y.astype(BF16), wglu_ref[...], preferred_element_type=F32) + bglu_ref[...]
        out = y * jax.nn.sigmoid(gate)
        for hf in range(out.shape[1] // LANES):
            y_scr[hf, r0:r0 + step, :] = out[:, hf * LANES:(hf + 1) * LANES]

    for c in range(nc):
        for hf in range(y_scr.shape[0]):
            o_ref[0, c * lc:(c + 1) * lc, hf * LANES:(hf + 1) * LANES] = (
                y_scr[hf, pl.ds(c, lc, stride=nc), :].astype(BF16))


def _s5(u, lam, laml, bblk, cblk, layer, d_row, wglu_blk, bglu_row):
    b, s, w = u.shape
    n = lam.shape[-1] // 2
    m_rows = S5_NCHUNK * S5_CHUNK
    full = lambda a: pl.BlockSpec(a.shape, lambda i, j: (0,) * a.ndim)
    per_layer = lambda a: pl.BlockSpec((1,) + a.shape[1:], lambda i, j: (layer,) + (0,) * (a.ndim - 1))
    return pl.pallas_call(
        functools.partial(_s5_kernel, n=n),
        grid=(b, s // m_rows),
        in_specs=[pl.BlockSpec((1, m_rows, LANES), lambda i, j: (i, j, 0)),
                  pl.BlockSpec((1, m_rows, LANES), lambda i, j: (i, j, 1)),
                  per_layer(lam), per_layer(laml), per_layer(bblk), per_layer(cblk),
                  full(d_row), full(wglu_blk), full(bglu_row)],
        out_specs=pl.BlockSpec((1, m_rows, w), lambda i, j: (i, j, 0)),
        out_shape=jax.ShapeDtypeStruct((b, s, w), BF16),
        scratch_shapes=[pltpu.VMEM((m_rows, w), F32),
                        pltpu.VMEM((m_rows, 2 * n), F32),
                        pltpu.VMEM((S5_NCHUNK, 2 * n), F32),
                        pltpu.VMEM((S5_NCHUNK, 2 * n), F32),
                        pltpu.VMEM((1, 2 * n), F32),
                        pltpu.VMEM((w // LANES, m_rows, LANES), F32)],
        compiler_params=_cparams(2),
        name="s5",
    )(u, u, lam, laml, bblk, cblk, d_row, wglu_blk, bglu_row)


def _fox_bias_tables(heads):
    sel = np.zeros((LANES, 2 * heads * LANES), np.float32)
    ones = np.zeros((1, 2 * heads * LANES), np.float32)
    koff = heads * LANES
    for h in range(heads):
        base = h * LANES + (HEAD_DIM if h % 2 == 0 else 0)
        for part in range(3):
            sel[part * heads + h, base + part] = 1.0
            ones[0, base + 3 + part] = 1.0
            ones[0, koff + base + part] = 1.0
            sel[part * heads + h, koff + base + 3 + part] = -1.0
    return jnp.asarray(sel, BF16), jnp.asarray(ones, F32)


def _foxprep_kernel(fl_ref, q_ref, k_ref, bf_ref, sel_ref, ones_ref, qa_ref, ka_ref, carry, cum_scr,
                    *, heads):
    @pl.when(pl.program_id(1) == 0)
    def _():
        carry[...] = jnp.zeros_like(carry)

    rows = fl_ref.shape[1]
    x = fl_ref[0] + bf_ref[...]
    lane = lax.broadcasted_iota(jnp.int32, (rows, LANES), 1)
    ls = jnp.minimum(x, 0.0) - jnp.log1p(jnp.exp(-jnp.abs(x)))
    ls = jnp.where(lane < heads, ls, 0.0)

    r = lax.broadcasted_iota(jnp.int32, (LANES, LANES), 0)
    c = lax.broadcasted_iota(jnp.int32, (LANES, LANES), 1)
    ltri = jnp.where(c <= r, 1.0, 0.0).astype(BF16)
    run = carry[...]
    for j in range(rows // LANES):
        blk = ls[j * LANES:(j + 1) * LANES]
        hi, mid, lo = _split3(blk)
        loc = (jnp.dot(ltri, hi, preferred_element_type=F32)
               + jnp.dot(ltri, mid, preferred_element_type=F32)
               + jnp.dot(ltri, lo, preferred_element_type=F32))
        cum_blk = loc + run
        run = cum_blk[LANES - 1:LANES, :]
        cum_scr[j * LANES:(j + 1) * LANES, :] = cum_blk
    carry[...] = run

    chi, cmid, clo = _split3(cum_scr[...] * LOG2E)
    c24 = (chi.astype(F32) + pltpu.roll(cmid.astype(F32), heads, 1)
           + pltpu.roll(clo.astype(F32), 2 * heads, 1)).astype(BF16)
    bias = jnp.dot(c24, sel_ref[...], preferred_element_type=F32) + ones_ref[...]
    bias = bias.astype(BF16)
    koff = heads * LANES
    for h in range(heads):
        pair = (h // 2) * LANES
        keep = (lane < HEAD_DIM) if h % 2 == 0 else (lane >= HEAD_DIM)
        qa_ref[0, h] = jnp.where(keep, q_ref[0, :, pair:pair + LANES],
                                 bias[:, h * LANES:(h + 1) * LANES])
        ka_ref[0, h] = jnp.where(keep, k_ref[0, :, pair:pair + LANES],
                                 bias[:, koff + h * LANES:koff + (h + 1) * LANES])


def _foxprep(fl, q, k, bf_row, heads):
    b, s, fw = q.shape
    tm = ROW_TILE
    sel, ones = _fox_bias_tables(heads)
    row = lambda n: pl.BlockSpec((1, tm, n), lambda i, j: (i, j, 0))
    full = lambda a: pl.BlockSpec(a.shape, lambda i, j: (0,) * a.ndim)
    aug = pl.BlockSpec((1, heads, tm, LANES), lambda i, j: (i, 0, j, 0))
    return pl.pallas_call(
        functools.partial(_foxprep_kernel, heads=heads),
        grid=(b, s // tm),
        in_specs=[row(LANES), row(fw), row(fw), full(bf_row), full(sel), full(ones)],
        out_specs=[aug, aug],
        out_shape=[jax.ShapeDtypeStruct((b, heads, s, LANES), BF16)] * 2,
        scratch_shapes=[pltpu.VMEM((1, LANES), F32), pltpu.VMEM((tm, LANES), F32)],
        compiler_params=_cparams(2),
        name="foxprep",
    )(fl, q, k, bf_row, sel, ones)


def _fox_kernel(qa_ref, ka_ref, v_ref, o_ref, va_scr):
    qi = pl.program_id(2)
    tq = qa_ref.shape[2]
    tk = tq
    nt = (((1,), (1,)), ((), ()))

    @pl.when(qi == 0)
    def _():
        lane_v = lax.broadcasted_iota(jnp.int32, va_scr.shape[1:], 1)
        v2 = v_ref[0]
        one = jnp.ones((), BF16)
        va_scr[0] = jnp.where(lane_v < HEAD_DIM, v2, one)
        va_scr[1] = jnp.where(lane_v >= HEAD_DIM, v2, one)

    row = lax.broadcasted_iota(jnp.int32, (tq, tk), 0)
    col = lax.broadcasted_iota(jnp.int32, (tq, tk), 1)
    causal = col <= row
    qs = [qa_ref[0, 0], qa_ref[0, 1]]

    def block(kj, state, masked):
        k0 = pl.multiple_of(kj * tk, tk)
        new = []
        for hh in range(2):
            m, acc = state[hh]
            kb = ka_ref[0, hh, pl.ds(k0, tk), :]
            s = lax.dot_general(qs[hh], kb, nt, preferred_element_type=F32)
            if masked:
                s = jnp.where(causal, s, NEG_INF)
            m_new = jnp.maximum(m, jnp.max(s, axis=-1, keepdims=True))
            alpha = jnp.exp2(m - m_new)
            p = jnp.exp2(s - m_new)
            pv = jnp.dot(p.astype(BF16), va_scr[hh, pl.ds(k0, tk), :], preferred_element_type=F32)
            new.append((m_new, alpha * acc + pv))
        return tuple(new)

    init = tuple((jnp.full((tq, 1), -jnp.inf, F32), jnp.zeros((tq, LANES), F32)) for _ in range(2))
    state = lax.fori_loop(0, qi, lambda kj, st: block(kj, st, False), init)
    state = block(qi, state, True)
    outs = [acc / pltpu.roll(acc, HEAD_DIM, 1) for _, acc in state]
    lane = lax.broadcasted_iota(jnp.int32, (tq, LANES), 1)
    o_ref[0] = jnp.where(lane < HEAD_DIM, outs[0], outs[1]).astype(BF16)


def _fox(q_aug, k_aug, v):
    b, heads, s, _ = q_aug.shape
    fw = v.shape[-1]
    tq = FOX_BLOCK
    return pl.pallas_call(
        _fox_kernel,
        grid=(b, heads // 2, s // tq),
        in_specs=[pl.BlockSpec((1, 2, tq, LANES), lambda i, p, j: (i, p, j, 0)),
                  pl.BlockSpec((1, 2, s, LANES), lambda i, p, j: (i, p, 0, 0)),
                  pl.BlockSpec((1, s, LANES), lambda i, p, j: (i, 0, p))],
        out_specs=pl.BlockSpec((1, tq, LANES), lambda i, p, j: (i, j, p)),
        out_shape=jax.ShapeDtypeStruct((b, s, fw), BF16),
        scratch_shapes=[pltpu.VMEM((2, s, LANES), BF16)],
        compiler_params=_cparams(3),
        name="fox",
    )(q_aug, k_aug, v)


def _outproj_kernel(ygm_ref, ys5_ref, yfox_ref, w_ref, x_ref, mod_ref, g_ref, b_ref, o_ref,
                    *, alpha, gm, s5w):
    mix = jnp.dot(ygm_ref[0], w_ref[0:gm, :], preferred_element_type=F32)
    mix += jnp.dot(ys5_ref[0], w_ref[gm:gm + s5w, :], preferred_element_type=F32)
    mix += jnp.dot(yfox_ref[0], w_ref[gm + s5w:, :], preferred_element_type=F32)
    gate = mod_ref[0, 2:3, :]
    y = alpha * x_ref[0] + (1.0 + gate) * mix
    o_ref[0] = _layer_norm_rows(y, g_ref[...], b_ref[...])


def _outproj(y_gm, y_s5, y_fox, w_out, x, mod_l, g_row, b_row, alpha):
    b, s, d = x.shape
    gm, s5w, fw = y_gm.shape[-1], y_s5.shape[-1], y_fox.shape[-1]
    tm = ROW_TILE
    row = lambda n: pl.BlockSpec((1, tm, n), lambda i, j: (i, j, 0))
    full = lambda a: pl.BlockSpec(a.shape, lambda i, j: (0,) * a.ndim)
    return pl.pallas_call(
        functools.partial(_outproj_kernel, alpha=alpha, gm=gm, s5w=s5w),
        grid=(b, s // tm),
        in_specs=[row(gm), row(s5w), row(fw), full(w_out), row(d),
                  pl.BlockSpec((1, 6, d), lambda i, j: (i, 0, 0)), full(g_row), full(b_row)],
        out_specs=row(d),
        out_shape=jax.ShapeDtypeStruct((b, s, d), F32),
        compiler_params=_cparams(2),
        name="outproj",
    )(y_gm, y_s5, y_fox, w_out, x, mod_l, g_row, b_row)


def _ffn_kernel(x_ref, mod_ref, wup_ref, cw_ref, cb_ref, wdn_ref, g_ref, b_ref, o_ref,
                h_scr, halo, acc, *, alpha, dff, chunk):
    @pl.when(pl.program_id(1) == 0)
    def _():
        halo[...] = jnp.zeros_like(halo)

    x = x_ref[0]
    rows = x.shape[0]
    sh = mod_ref[0, 3:4, :]
    sc = mod_ref[0, 4:5, :]
    gate = mod_ref[0, 5:6, :]
    h_scr[...] = (x * (1.0 + sc) + sh).astype(BF16)
    h = h_scr[...]
    rid = lax.broadcasted_iota(jnp.int32, (rows, chunk), 0)
    for ci in range(dff // chunk):
        c0 = ci * chunk
        a = jnp.dot(h, wup_ref[:, c0:c0 + chunk], preferred_element_type=F32)
        gt = jnp.dot(h, wup_ref[:, dff + c0:dff + c0 + chunk], preferred_element_type=F32)
        p1 = halo[SUBLANES - 1:SUBLANES, c0:c0 + chunk]
        p2 = halo[SUBLANES - 2:SUBLANES - 1, c0:c0 + chunk]
        a1 = jnp.where(rid == 0, p1, pltpu.roll(a, 1, 0))
        a2 = jnp.where(rid == 0, p2, jnp.where(rid == 1, p1, pltpu.roll(a, 2, 0)))
        halo[:, c0:c0 + chunk] = a[rows - SUBLANES:rows, :]
        conv = (cb_ref[:, c0:c0 + chunk] + cw_ref[0:1, c0:c0 + chunk] * a2
                + cw_ref[1:2, c0:c0 + chunk] * a1 + cw_ref[2:3, c0:c0 + chunk] * a)
        act = (_gelu_tanh(conv) * gt).astype(BF16)
        part = jnp.dot(act, wdn_ref[c0:c0 + chunk, :], preferred_element_type=F32)
        if ci == 0:
            acc[...] = part
        else:
            acc[...] += part
    y = alpha * x + (1.0 + gate) * acc[...]
    o_ref[0] = _layer_norm_rows(y, g_ref[...], b_ref[...])


def _ffn(x, mod_l, w_up, conv_w, conv_b_row, w_down, g_row, b_row, alpha):
    b, s, d = x.shape
    dff = w_down.shape[0]
    tm = ROW_TILE
    chunk = 256
    row = pl.BlockSpec((1, tm, d), lambda i, j: (i, j, 0))
    full = lambda a: pl.BlockSpec(a.shape, lambda i, j: (0,) * a.ndim)
    resident = lambda a: pl.BlockSpec(a.shape, lambda i, j: (0,) * a.ndim,
                                      pipeline_mode=pl.Buffered(1))
    return pl.pallas_call(
        functools.partial(_ffn_kernel, alpha=alpha, dff=dff, chunk=chunk),
        grid=(b, s // tm),
        in_specs=[row, pl.BlockSpec((1, 6, d), lambda i, j: (i, 0, 0)),
                  resident(w_up), full(conv_w), full(conv_b_row), resident(w_down),
                  full(g_row), full(b_row)],
        out_specs=row,
        out_shape=jax.ShapeDtypeStruct((b, s, d), F32),
        scratch_shapes=[pltpu.VMEM((tm, d), BF16),
                        pltpu.VMEM((SUBLANES, dff), F32),
                        pltpu.VMEM((tm, d), F32)],
        compiler_params=_cparams(2),
        name="ffn",
    )(x, mod_l, w_up, conv_w, conv_b_row, w_down, g_row, b_row)


def _block_diag(a):
    depth, g, r, c = a.shape
    eye = jnp.eye(g, dtype=a.dtype)
    return (a[:, :, :, None, :] * eye[None, :, None, :, None]).reshape(depth, g * r, g * c)


def kernel(x, c, w_ada, b_ada, w_in, b_f, gm_ln_g, gm_ln_b, gm_w_s, gm_b_s, s5_lam_re, s5_lam_im, s5_log_dt, s5_b_re, s5_b_im, s5_c_re, s5_c_im, s5_d, s5_w_glu, s5_b_glu, w_out, ln1_g, ln1_b, w_up, conv_w, conv_b, w_down, ln2_g, ln2_b):
    depth, d, _ = w_in.shape
    batch = x.shape[0]
    gm_heads = gm_w_s.shape[1]
    gm = gm_heads * HEAD_DIM
    groups = s5_lam_re.shape[1]
    s5w = groups * S5_GROUP_DIM
    fox_heads = b_f.shape[1]
    fw = fox_heads * HEAD_DIM
    n_main = 2 * gm + s5w + 3 * fw
    alpha = (2.0 * depth) ** 0.25

    pad_rows = 2 * SUBLANES
    c_pad = jnp.pad(c, ((0, pad_rows - batch), (0, 0)))
    mod = _adaln(c_pad, w_ada, b_ada[:, None, :])[:, :batch]
    mod = mod.reshape(depth, batch, 6, d)

    n_state = groups * S5_STATE
    lre = s5_lam_re.reshape(depth, 1, n_state)
    lim = s5_lam_im.reshape(depth, 1, n_state)
    ldt = jnp.repeat(s5_log_dt, S5_STATE, axis=-1).reshape(depth, 1, n_state)
    bre_blk = _block_diag(jnp.swapaxes(s5_b_re, -1, -2))
    bim_blk = _block_diag(jnp.swapaxes(s5_b_im, -1, -2))
    cre_blk = _block_diag(jnp.swapaxes(s5_c_re, -1, -2))
    cim_blk = _block_diag(jnp.swapaxes(s5_c_im, -1, -2))
    lam, laml, bblk, cblk = _s5prep(lre, lim, ldt, bre_blk, bim_blk, cre_blk, cim_blk)
    wglu_blk = _block_diag(jnp.swapaxes(s5_w_glu, -1, -2)).astype(BF16)

    w_in_bf = w_in.astype(BF16)
    w_out_bf = w_out.astype(BF16)
    w_up_bf = w_up.astype(BF16)
    w_down_bf = w_down.astype(BF16)

    for l in range(depth):
        mod_l = mod[l]
        w_main = w_in_bf[l, :, :n_main]
        w_f = jnp.pad(w_in_bf[l, :, n_main:], ((0, 0), (0, LANES - fox_heads)))
        gm_u, gm_v, s5_in, q, k, v, fl = _inproj(x, mod_l, w_main, w_f, gm, s5w, fw)

        bs_full = jnp.repeat(gm_b_s[l].T, HEAD_DIM, axis=1)
        y_gm = _gmlp(gm_u, gm_v, gm_ln_g[l].reshape(1, gm), gm_ln_b[l].reshape(1, gm),
                     gm_w_s[l], bs_full)

        y_s5 = _s5(s5_in, lam, laml, bblk, cblk, l, s5_d[l].reshape(1, s5w), wglu_blk[l],
                   s5_b_glu[l].reshape(1, s5w))

        bf_row = jnp.pad(b_f[l][None, :], ((0, 0), (0, LANES - fox_heads)))
        q_aug, k_aug = _foxprep(fl, q, k, bf_row, fox_heads)
        y_fox = _fox(q_aug, k_aug, v)

        x = _outproj(y_gm, y_s5, y_fox, w_out_bf[l], x, mod_l, ln1_g[l][None, :], ln1_b[l][None, :],
                     alpha)
        x = _ffn(x, mod_l, w_up_bf[l], conv_w[l], conv_b[l][None, :], w_down_bf[l],
                 ln2_g[l][None, :], ln2_b[l][None, :], alpha)
    return x
```

```python
import functools
import math

import numpy as np
import jax
import jax.numpy as jnp
from jax import lax
from jax.experimental import pallas as pl
from jax.experimental.pallas import tpu as pltpu

F32 = jnp.float32
BF16 = jnp.bfloat16

HEAD_DIM = 64
GM_CHUNK = 128
S5_GROUP_DIM = 16
S5_STATE = 64
CONV_WIDTH = 3
LN_EPS = 1e-5
NEG_INF = -1e30
LOG2E = math.log2(math.e)

LANES = 128
SUBLANES = 8
ROW_TILE = 512
FFN_ROW_TILE = 512
S5_CHUNK = 128
S5_NCHUNK = SUBLANES
FOX_BLOCK = 512
FOX_HEADS_PER_STEP = 4
VMEM_LIMIT = 56 * 1024 * 1024


def _cparams(n_axes, vmem=VMEM_LIMIT):
    return pltpu.CompilerParams(dimension_semantics=("arbitrary",) * n_axes,
                                vmem_limit_bytes=vmem)


def _layer_norm_rows(y, g, b):
    mu = jnp.mean(y, axis=-1, keepdims=True)
    d = y - mu
    var = jnp.mean(d * d, axis=-1, keepdims=True)
    return d * lax.rsqrt(var + LN_EPS) * g + b


def _gelu_tanh(x):
    c = math.sqrt(2.0 / math.pi)
    return 0.5 * x * (1.0 + jnp.tanh(c * (x + 0.044715 * (x * x * x))))


def _split3(x):
    hi = x.astype(BF16)
    r1 = x - hi.astype(F32)
    mid = r1.astype(BF16)
    lo = (r1 - mid.astype(F32)).astype(BF16)
    return hi, mid, lo


def _adaln_kernel(c_ref, w_ref, b_ref, o_ref):
    c = c_ref[...]
    cond = (c * jax.nn.sigmoid(c)).astype(BF16)
    w = w_ref[0].astype(BF16)
    o_ref[0] = jnp.dot(cond, w, preferred_element_type=F32) + b_ref[0]


def _adaln(c_pad, w_ada, b_ada3):
    depth, d, n = w_ada.shape
    rows = c_pad.shape[0]
    tn = 1536
    return pl.pallas_call(
        _adaln_kernel,
        grid=(depth, n // tn),
        in_specs=[pl.BlockSpec((rows, d), lambda l, j: (0, 0)),
                  pl.BlockSpec((1, d, tn), lambda l, j: (l, 0, j)),
                  pl.BlockSpec((1, 1, tn), lambda l, j: (l, 0, j))],
        out_specs=pl.BlockSpec((1, rows, tn), lambda l, j: (l, 0, j)),
        out_shape=jax.ShapeDtypeStruct((depth, rows, n), F32),
        compiler_params=_cparams(2),
        name="adaln",
    )(c_pad, w_ada, b_ada3)


def _inproj_kernel(x_ref, mod_ref, w_ref, wf_ref,
                   gu_ref, gv_ref, s5_ref, q_ref, k_ref, v_ref, fl_ref, h_scr, *, gm, s5w, fw):
    sh = mod_ref[0, 0:1, :]
    sc = mod_ref[0, 1:2, :]
    h_scr[...] = (x_ref[0] * (1.0 + sc) + sh).astype(BF16)
    h = h_scr[...]

    def proj(lo, hi):
        return jnp.dot(h, w_ref[:, lo:hi], preferred_element_type=F32)

    o = 0
    gu_ref[0] = proj(o, o + gm); o += gm
    gv_ref[0] = proj(o, o + gm); o += gm
    s5_ref[0] = proj(o, o + s5w); o += s5w
    q_ref[0] = (proj(o, o + fw) * (LOG2E * HEAD_DIM ** -0.5)).astype(BF16); o += fw
    k_ref[0] = proj(o, o + fw).astype(BF16); o += fw
    v_ref[0] = proj(o, o + fw).astype(BF16)
    fl_ref[0] = jnp.dot(h, wf_ref[...], preferred_element_type=F32)


def _layer_spec(a, layer, block=None):
    shape = tuple(a.shape[1:]) if block is None else tuple(block)
    return pl.BlockSpec((None,) + shape, lambda *_: (layer,) + (0,) * len(shape))


def _mod_spec(mod, layer):
    return pl.BlockSpec((None, 1) + tuple(mod.shape[2:]), lambda i, *_: (layer, i, 0, 0))


def _inproj(x, mod, w_in, w_f, layer, gm, s5w, fw):
    b, s, d = x.shape
    tm = ROW_TILE
    n_main = 2 * gm + s5w + 3 * fw
    row = lambda n: pl.BlockSpec((1, tm, n), lambda i, j: (i, j, 0))
    outs = [jax.ShapeDtypeStruct((b, s, gm), F32), jax.ShapeDtypeStruct((b, s, gm), F32),
            jax.ShapeDtypeStruct((b, s, s5w), F32), jax.ShapeDtypeStruct((b, s, fw), BF16),
            jax.ShapeDtypeStruct((b, s, fw), BF16), jax.ShapeDtypeStruct((b, s, fw), BF16),
            jax.ShapeDtypeStruct((b, s, LANES), F32)]
    return pl.pallas_call(
        functools.partial(_inproj_kernel, gm=gm, s5w=s5w, fw=fw),
        grid=(b, s // tm),
        in_specs=[row(d), _mod_spec(mod, layer),
                  _layer_spec(w_in, layer, (d, n_main)), _layer_spec(w_f, layer)],
        out_specs=[row(gm), row(gm), row(s5w), row(fw), row(fw), row(fw), row(LANES)],
        out_shape=outs,
        scratch_shapes=[pltpu.VMEM((tm, d), BF16)],
        compiler_params=_cparams(2),
        name="inproj",
    )(x, mod, w_in, w_f)


def _gmlp_kernel(u_ref, v_ref, g_ref, b_ref, ws_ref, bs_ref, o_ref, *, heads):
    v = v_ref[0]
    rows, width = v.shape
    lane = lax.broadcasted_iota(jnp.int32, (rows, width), 1)
    head_masks = [(lane >= h * HEAD_DIM) & (lane < (h + 1) * HEAD_DIM) for h in range(heads)]

    def seg_mean(a):
        out = jnp.zeros_like(a)
        for m in head_masks:
            s = jnp.sum(jnp.where(m, a, 0.0), axis=-1, keepdims=True) * (1.0 / HEAD_DIM)
            out = jnp.where(m, s, out)
        return out

    d = v - seg_mean(v)
    var = seg_mean(d * d)
    vn = (d * lax.rsqrt(var + LN_EPS) * g_ref[...] + b_ref[...]).astype(BF16)

    r = lax.broadcasted_iota(jnp.int32, (GM_CHUNK, GM_CHUNK), 0)
    c = lax.broadcasted_iota(jnp.int32, (GM_CHUNK, GM_CHUNK), 1)
    tril = c <= r
    ws = [jnp.where(tril, ws_ref[h], 0.0).astype(BF16) for h in range(heads)]
    lane_c = lax.broadcasted_iota(jnp.int32, (GM_CHUNK, width), 1)
    chunk_masks = [(lane_c >= h * HEAD_DIM) & (lane_c < (h + 1) * HEAD_DIM) for h in range(heads)]
    bs = bs_ref[...]
    for j in range(rows // GM_CHUNK):
        r0 = j * GM_CHUNK
        vc = vn[r0:r0 + GM_CHUNK]
        z = jnp.zeros((GM_CHUNK, width), F32)
        for h in range(heads):
            zh = jnp.dot(ws[h], vc, preferred_element_type=F32)
            z = jnp.where(chunk_masks[h], zh, z)
        o_ref[0, r0:r0 + GM_CHUNK, :] = (u_ref[0, r0:r0 + GM_CHUNK, :] * (z + bs)).astype(BF16)


def _gmlp(gm_u, gm_v, ln_g_row, ln_b_row, w_s, b_s_full, layer):
    b, s, w = gm_u.shape
    heads = w_s.shape[1]
    tm = ROW_TILE
    row = pl.BlockSpec((1, tm, w), lambda i, j: (i, j, 0))
    return pl.pallas_call(
        functools.partial(_gmlp_kernel, heads=heads),
        grid=(b, s // tm),
        in_specs=[row, row, _layer_spec(ln_g_row, layer), _layer_spec(ln_b_row, layer),
                  _layer_spec(w_s, layer), _layer_spec(b_s_full, layer)],
        out_specs=row,
        out_shape=jax.ShapeDtypeStruct((b, s, w), BF16),
        compiler_params=_cparams(2),
        name="gmlp",
    )(gm_u, gm_v, ln_g_row, ln_b_row, w_s, b_s_full)


def _s5prep_kernel(lre_ref, lim_ref, ldt_ref, bre_ref, bim_ref, cre_ref, cim_ref,
                   lam_ref, laml_ref, bblk_ref, cblk_ref):
    lre = lre_ref[0]
    lim = lim_ref[0]
    dt = jnp.exp(ldt_ref[0])
    mag = jnp.exp(lre * dt)
    lbr = mag * jnp.cos(lim * dt)
    lbi = mag * jnp.sin(lim * dt)
    nr = lbr - 1.0
    den = lre * lre + lim * lim
    cr = (nr * lre + lbi * lim) / den
    ci = (lbi * lre - nr * lim) / den
    n = lre.shape[-1]
    lam_ref[0, :, 0:n] = lbr
    lam_ref[0, :, n:2 * n] = lbi
    pr, pi = lbr, lbi
    for _ in range(int(math.log2(S5_CHUNK))):
        pr, pi = pr * pr - pi * pi, 2.0 * (pr * pi)
    laml_ref[0, :, 0:n] = pr
    laml_ref[0, :, n:2 * n] = pi
    bre = bre_ref[0]
    bim = bim_ref[0]
    bblk_ref[0, :, 0:n] = (cr * bre - ci * bim).astype(BF16)
    bblk_ref[0, :, n:2 * n] = (cr * bim + ci * bre).astype(BF16)
    cblk_ref[0, 0:n, :] = cre_ref[0].astype(BF16)
    cblk_ref[0, n:2 * n, :] = (-cim_ref[0]).astype(BF16)


def _s5prep(lre, lim, ldt, bre_blk, bim_blk, cre_blk, cim_blk):
    depth, _, n = lre.shape
    w = bre_blk.shape[1]
    vec = pl.BlockSpec((1, 1, n), lambda l: (l, 0, 0))
    return pl.pallas_call(
        _s5prep_kernel,
        grid=(depth,),
        in_specs=[vec, vec, vec,
                  pl.BlockSpec((1, w, n), lambda l: (l, 0, 0)),
                  pl.BlockSpec((1, w, n), lambda l: (l, 0, 0)),
                  pl.BlockSpec((1, n, w), lambda l: (l, 0, 0)),
                  pl.BlockSpec((1, n, w), lambda l: (l, 0, 0))],
        out_specs=[pl.BlockSpec((1, 1, 2 * n), lambda l: (l, 0, 0)),
                   pl.BlockSpec((1, 1, 2 * n), lambda l: (l, 0, 0)),
                   pl.BlockSpec((1, w, 2 * n), lambda l: (l, 0, 0)),
                   pl.BlockSpec((1, 2 * n, w), lambda l: (l, 0, 0))],
        out_shape=[jax.ShapeDtypeStruct((depth, 1, 2 * n), F32),
                   jax.ShapeDtypeStruct((depth, 1, 2 * n), F32),
                   jax.ShapeDtypeStruct((depth, w, 2 * n), BF16),
                   jax.ShapeDtypeStruct((depth, 2 * n, w), BF16)],
        compiler_params=_cparams(1),
        name="s5prep",
    )(lre, lim, ldt, bre_blk, bim_blk, cre_blk, cim_blk)


def _s5_kernel(ua_ref, ub_ref, lam_ref, laml_ref, bblk_ref, cblk_ref, d_ref, wglu_ref, bglu_ref,
               o_ref, uperm, r_scr, e_scr, xin_scr, carry, y_scr, *, n):
    nc, lc = S5_NCHUNK, S5_CHUNK
    m_rows = nc * lc

    @pl.when(pl.program_id(1) == 0)
    def _():
        carry[...] = jnp.zeros_like(carry)

    def permute(t, _):
        r0 = pl.multiple_of(t * nc, nc)
        uperm[pl.ds(r0, nc), 0:LANES] = ua_ref[0, pl.ds(t, nc, stride=lc), :]
        uperm[pl.ds(r0, nc), LANES:2 * LANES] = ub_ref[0, pl.ds(t, nc, stride=lc), :]
        return 0
    lax.fori_loop(0, lc, permute, 0)

    up = uperm[...]
    r_scr[...] = jnp.dot(up.astype(BF16), bblk_ref[0], preferred_element_type=F32)

    lr = jnp.broadcast_to(lam_ref[0, :, 0:n], (nc, n))
    li = jnp.broadcast_to(lam_ref[0, :, n:2 * n], (nc, n))

    def scan(x0r, x0i, store):
        def body(t, c):
            xr, xi = c
            r0 = pl.multiple_of(t * nc, nc)
            br = r_scr[pl.ds(r0, nc), 0:n]
            bi = r_scr[pl.ds(r0, nc), n:2 * n]
            nxr = lr * xr - li * xi + br
            nxi = lr * xi + li * xr + bi
            if store:
                r_scr[pl.ds(r0, nc), 0:n] = nxr
                r_scr[pl.ds(r0, nc), n:2 * n] = nxi
            return nxr, nxi
        return lax.fori_loop(0, lc, body, (x0r, x0i), unroll=2)

    zero = jnp.zeros((nc, n), F32)
    er, ei = scan(zero, zero, False)
    e_scr[:, 0:n] = er
    e_scr[:, n:2 * n] = ei

    llr = laml_ref[0, :, 0:n]
    lli = laml_ref[0, :, n:2 * n]
    cur_r = carry[:, 0:n]
    cur_i = carry[:, n:2 * n]
    for c in range(nc):
        xin_scr[c:c + 1, 0:n] = cur_r
        xin_scr[c:c + 1, n:2 * n] = cur_i
        e_r = e_scr[c:c + 1, 0:n]
        e_i = e_scr[c:c + 1, n:2 * n]
        cur_r, cur_i = llr * cur_r - lli * cur_i + e_r, llr * cur_i + lli * cur_r + e_i
    carry[:, 0:n] = cur_r
    carry[:, n:2 * n] = cur_i

    scan(xin_scr[:, 0:n], xin_scr[:, n:2 * n], True)

    step = 256
    for r0 in range(0, m_rows, step):
        xs = r_scr[r0:r0 + step, :].astype(BF16)
        y = jnp.dot(xs, cblk_ref[0], preferred_element_type=F32)
        y = y + d_ref[...] * uperm[r0:r0 + step, :]
        y = _gelu_tanh(y)
        gate = jnp.dot(y.astype(BF16), wglu_ref[...], preferred_element_type=F32) + bglu_ref[...]
        out = y * jax.nn.sigmoid(gate)
        for hf in range(out.shape[1] // LANES):
            y_scr[hf, r0:r0 + step, :] = out[:, hf * LANES:(hf + 1) * LANES]

    for c in range(nc):
        for hf in range(y_scr.shape[0]):
            o_ref[0, c * lc:(c + 1) * lc, hf * LANES:(hf + 1) * LANES] = (
                y_scr[hf, pl.ds(c, lc, stride=nc), :].astype(BF16))


def _s5(u, lam, laml, bblk, cblk, layer, d_row, wglu_blk, bglu_row):
    b, s, w = u.shape
    n = lam.shape[-1] // 2
    m_rows = S5_NCHUNK * S5_CHUNK
    per_layer = lambda a: pl.BlockSpec((1,) + a.shape[1:], lambda i, j: (layer,) + (0,) * (a.ndim - 1))
    return pl.pallas_call(
        functools.partial(_s5_kernel, n=n),
        grid=(b, s // m_rows),
        in_specs=[pl.BlockSpec((1, m_rows, LANES), lambda i, j: (i, j, 0)),
                  pl.BlockSpec((1, m_rows, LANES), lambda i, j: (i, j, 1)),
                  per_layer(lam), per_layer(laml), per_layer(bblk), per_layer(cblk),
                  _layer_spec(d_row, layer), _layer_spec(wglu_blk, layer),
                  _layer_spec(bglu_row, layer)],
        out_specs=pl.BlockSpec((1, m_rows, w), lambda i, j: (i, j, 0)),
        out_shape=jax.ShapeDtypeStruct((b, s, w), BF16),
        scratch_shapes=[pltpu.VMEM((m_rows, w), F32),
                        pltpu.VMEM((m_rows, 2 * n), F32),
                        pltpu.VMEM((S5_NCHUNK, 2 * n), F32),
                        pltpu.VMEM((S5_NCHUNK, 2 * n), F32),
                        pltpu.VMEM((1, 2 * n), F32),
                        pltpu.VMEM((w // LANES, m_rows, LANES), F32)],
        compiler_params=_cparams(2),
        name="s5",
    )(u, u, lam, laml, bblk, cblk, d_row, wglu_blk, bglu_row)


def _fox_bias_tables(heads):
    sel = np.zeros((LANES, 2 * heads * LANES), np.float32)
    ones = np.zeros((1, 2 * heads * LANES), np.float32)
    koff = heads * LANES
    for h in range(heads):
        base = h * LANES + (HEAD_DIM if h % 2 == 0 else 0)
        for part in range(3):
            sel[part * heads + h, base + part] = 1.0
            ones[0, base + 3 + part] = 1.0
            ones[0, koff + base + part] = 1.0
            sel[part * heads + h, koff + base + 3 + part] = -1.0
    return jnp.asarray(sel, BF16), jnp.asarray(ones, F32)


def _foxprep_kernel(fl_ref, q_ref, k_ref, bf_ref, sel_ref, ones_ref, qa_ref, ka_ref, carry, cum_scr,
                    *, heads):
    @pl.when(pl.program_id(1) == 0)
    def _():
        carry[...] = jnp.zeros_like(carry)

    rows = fl_ref.shape[1]
    x = fl_ref[0] + bf_ref[...]
    lane = lax.broadcasted_iota(jnp.int32, (rows, LANES), 1)
    ls = jnp.minimum(x, 0.0) - jnp.log1p(jnp.exp(-jnp.abs(x)))
    ls = jnp.where(lane < heads, ls, 0.0)

    r = lax.broadcasted_iota(jnp.int32, (LANES, LANES), 0)
    c = lax.broadcasted_iota(jnp.int32, (LANES, LANES), 1)
    ltri = jnp.where(c <= r, 1.0, 0.0).astype(BF16)
    run = carry[...]
    for j in range(rows // LANES):
        blk = ls[j * LANES:(j + 1) * LANES]
        hi, mid, lo = _split3(blk)
        loc = (jnp.dot(ltri, hi, preferred_element_type=F32)
               + jnp.dot(ltri, mid, preferred_element_type=F32)
               + jnp.dot(ltri, lo, preferred_element_type=F32))
        cum_blk = loc + run
        run = cum_blk[LANES - 1:LANES, :]
        cum_scr[j * LANES:(j + 1) * LANES, :] = cum_blk
    carry[...] = run

    chi, cmid, clo = _split3(cum_scr[...] * LOG2E)
    c24 = (chi.astype(F32) + pltpu.roll(cmid.astype(F32), heads, 1)
           + pltpu.roll(clo.astype(F32), 2 * heads, 1)).astype(BF16)
    bias = jnp.dot(c24, sel_ref[...], preferred_element_type=F32) + ones_ref[...]
    bias = bias.astype(BF16)
    koff = heads * LANES
    for h in range(heads):
        pair = (h // 2) * LANES
        keep = (lane < HEAD_DIM) if h % 2 == 0 else (lane >= HEAD_DIM)
        qa_ref[0, h] = jnp.where(keep, q_ref[0, :, pair:pair + LANES],
                                 bias[:, h * LANES:(h + 1) * LANES])
        ka_ref[0, h] = jnp.where(keep, k_ref[0, :, pair:pair + LANES],
                                 bias[:, koff + h * LANES:koff + (h + 1) * LANES])


def _foxprep(fl, q, k, bf_row, sel, ones, layer, heads):
    b, s, fw = q.shape
    tm = ROW_TILE
    row = lambda n: pl.BlockSpec((1, tm, n), lambda i, j: (i, j, 0))
    full = lambda a: pl.BlockSpec(a.shape, lambda i, j: (0,) * a.ndim)
    aug = pl.BlockSpec((1, heads, tm, LANES), lambda i, j: (i, 0, j, 0))
    return pl.pallas_call(
        functools.partial(_foxprep_kernel, heads=heads),
        grid=(b, s // tm),
        in_specs=[row(LANES), row(fw), row(fw), _layer_spec(bf_row, layer), full(sel), full(ones)],
        out_specs=[aug, aug],
        out_shape=[jax.ShapeDtypeStruct((b, heads, s, LANES), BF16)] * 2,
        scratch_shapes=[pltpu.VMEM((1, LANES), F32), pltpu.VMEM((tm, LANES), F32)],
        compiler_params=_cparams(2),
        name="foxprep",
    )(fl, q, k, bf_row, sel, ones)


def _fox_kernel(qa_ref, ka_ref, v_ref, o_ref, va_scr):
    qi = pl.program_id(2)
    nh = qa_ref.shape[1]
    tq = qa_ref.shape[2]
    tk = tq
    nt = (((1,), (1,)), ((), ()))

    @pl.when(qi == 0)
    def _():
        lane_v = lax.broadcasted_iota(jnp.int32, va_scr.shape[1:], 1)
        one = jnp.ones((), BF16)
        for hp in range(nh // 2):
            v2 = v_ref[0, :, hp * LANES:(hp + 1) * LANES]
            va_scr[2 * hp] = jnp.where(lane_v < HEAD_DIM, v2, one)
            va_scr[2 * hp + 1] = jnp.where(lane_v >= HEAD_DIM, v2, one)

    def step(k0, width, state, mask_off):
        new = []
        for hh in range(nh):
            m, acc = state[hh]
            kb = ka_ref[0, hh, pl.ds(k0, width), :]
            s = lax.dot_general(qa_ref[0, hh], kb, nt, preferred_element_type=F32)
            if mask_off is not None:
                row = lax.broadcasted_iota(jnp.int32, (tq, width), 0)
                col = lax.broadcasted_iota(jnp.int32, (tq, width), 1)
                s = jnp.where(col <= row + mask_off, s, NEG_INF)
            m_new = jnp.maximum(m, jnp.max(s, axis=-1, keepdims=True))
            alpha = jnp.exp2(m - m_new)
            p = jnp.exp2(s - m_new)
            pv = jnp.dot(p.astype(BF16), va_scr[hh, pl.ds(k0, width), :],
                         preferred_element_type=F32)
            new.append((m_new, alpha * acc + pv))
        return tuple(new)

    wide = 2 * tk
    n_wide = qi // 2
    init = tuple((jnp.full((tq, 1), -jnp.inf, F32), jnp.zeros((tq, LANES), F32))
                 for _ in range(nh))
    state = lax.fori_loop(
        0, n_wide, lambda j, st: step(pl.multiple_of(j * wide, wide), wide, st, None), init)
    k_tail = pl.multiple_of(n_wide * wide, wide)
    state = lax.cond(qi % 2 == 1,
                     lambda st: step(k_tail, wide, st, tk),
                     lambda st: step(k_tail, tk, st, 0),
                     state)
    outs = [acc / pltpu.roll(acc, HEAD_DIM, 1) for _, acc in state]
    lane = lax.broadcasted_iota(jnp.int32, (tq, LANES), 1)
    for hp in range(nh // 2):
        o_ref[0, :, hp * LANES:(hp + 1) * LANES] = jnp.where(
            lane < HEAD_DIM, outs[2 * hp], outs[2 * hp + 1]).astype(BF16)


def _fox(q_aug, k_aug, v):
    b, heads, s, _ = q_aug.shape
    fw = v.shape[-1]
    tq = FOX_BLOCK
    nh = FOX_HEADS_PER_STEP
    vw = nh // 2 * LANES
    return pl.pallas_call(
        _fox_kernel,
        grid=(b, heads // nh, s // tq),
        in_specs=[pl.BlockSpec((1, nh, tq, LANES), lambda i, p, j: (i, p, j, 0)),
                  pl.BlockSpec((1, nh, s, LANES), lambda i, p, j: (i, p, 0, 0)),
                  pl.BlockSpec((1, s, vw), lambda i, p, j: (i, 0, p))],
        out_specs=pl.BlockSpec((1, tq, vw), lambda i, p, j: (i, j, p)),
        out_shape=jax.ShapeDtypeStruct((b, s, fw), BF16),
        scratch_shapes=[pltpu.VMEM((nh, s, LANES), BF16)],
        compiler_params=_cparams(3),
        name="fox",
    )(q_aug, k_aug, v)


def _mixffn_kernel(ygm_ref, ys5_ref, yfox_ref, wout_ref, x_ref, mod_ref, g1_ref, b1_ref,
                   wup_ref, cw_ref, cb_ref, wdn_ref, g2_ref, b2_ref, o_ref,
                   x1_scr, h_scr, halo, act_scr, *, alpha, gm, s5w, dff, chunk):
    @pl.when(pl.program_id(1) == 0)
    def _():
        halo[...] = jnp.zeros_like(halo)

    mix = jnp.dot(ygm_ref[0], wout_ref[0:gm, :], preferred_element_type=F32)
    mix += jnp.dot(ys5_ref[0], wout_ref[gm:gm + s5w, :], preferred_element_type=F32)
    mix += jnp.dot(yfox_ref[0], wout_ref[gm + s5w:, :], preferred_element_type=F32)
    gate1 = mod_ref[0, 2:3, :]
    x1_scr[...] = _layer_norm_rows(alpha * x_ref[0] + (1.0 + gate1) * mix, g1_ref[...], b1_ref[...])

    rows = x_ref.shape[1]
    sh = mod_ref[0, 3:4, :]
    sc = mod_ref[0, 4:5, :]
    gate = mod_ref[0, 5:6, :]
    h_scr[...] = (x1_scr[...] * (1.0 + sc) + sh).astype(BF16)
    h = h_scr[...]
    rid = lax.broadcasted_iota(jnp.int32, (rows, chunk), 0)
    for ci in range(dff // chunk):
        c0 = ci * chunk
        a = jnp.dot(h, wup_ref[:, c0:c0 + chunk], preferred_element_type=F32)
        gt = jnp.dot(h, wup_ref[:, dff + c0:dff + c0 + chunk], preferred_element_type=F32)
        p1 = halo[SUBLANES - 1:SUBLANES, c0:c0 + chunk]
        p2 = halo[SUBLANES - 2:SUBLANES - 1, c0:c0 + chunk]
        a1 = jnp.where(rid == 0, p1, pltpu.roll(a, 1, 0))
        a2 = jnp.where(rid == 0, p2, jnp.where(rid == 1, p1, pltpu.roll(a, 2, 0)))
        halo[:, c0:c0 + chunk] = a[rows - SUBLANES:rows, :]
        conv = (cb_ref[:, c0:c0 + chunk] + cw_ref[0:1, c0:c0 + chunk] * a2
                + cw_ref[1:2, c0:c0 + chunk] * a1 + cw_ref[2:3, c0:c0 + chunk] * a)
        act_scr[:, c0:c0 + chunk] = (_gelu_tanh(conv) * gt).astype(BF16)
    ffn = jnp.dot(act_scr[...], wdn_ref[...], preferred_element_type=F32)
    y = alpha * x1_scr[...] + (1.0 + gate) * ffn
    o_ref[0] = _layer_norm_rows(y, g2_ref[...], b2_ref[...])


def _mixffn(y_gm, y_s5, y_fox, w_out, x, mod, g1_row, b1_row,
            w_up, conv_w, conv_b_row, w_down, g2_row, b2_row, layer, alpha):
    b, s, d = x.shape
    gm, s5w, fw = y_gm.shape[-1], y_s5.shape[-1], y_fox.shape[-1]
    dff = w_down.shape[1]
    tm = FFN_ROW_TILE
    chunk = 256
    row = lambda n: pl.BlockSpec((1, tm, n), lambda i, j: (i, j, 0))
    full = lambda a: _layer_spec(a, layer)
    resident = lambda a: pl.BlockSpec((None,) + a.shape[1:], lambda i, j: (layer,) + (0,) * (a.ndim - 1),
                                      pipeline_mode=pl.Buffered(1))
    return pl.pallas_call(
        functools.partial(_mixffn_kernel, alpha=alpha, gm=gm, s5w=s5w, dff=dff, chunk=chunk),
        grid=(b, s // tm),
        in_specs=[row(gm), row(s5w), row(fw), resident(w_out), row(d),
                  _mod_spec(mod, layer), full(g1_row), full(b1_row),
                  resident(w_up), full(conv_w), full(conv_b_row), resident(w_down),
                  full(g2_row), full(b2_row)],
        out_specs=row(d),
        out_shape=jax.ShapeDtypeStruct((b, s, d), F32),
        scratch_shapes=[pltpu.VMEM((tm, d), F32),
                        pltpu.VMEM((tm, d), BF16),
                        pltpu.VMEM((SUBLANES, dff), F32),
                        pltpu.VMEM((tm, dff), BF16)],
        compiler_params=_cparams(2),
        name="mixffn",
    )(y_gm, y_s5, y_fox, w_out, x, mod, g1_row, b1_row,
      w_up, conv_w, conv_b_row, w_down, g2_row, b2_row)


def _block_diag(a):
    depth, g, r, c = a.shape
    eye = jnp.eye(g, dtype=a.dtype)
    return (a[:, :, :, None, :] * eye[None, :, None, :, None]).reshape(depth, g * r, g * c)


def kernel(x, c, w_ada, b_ada, w_in, b_f, gm_ln_g, gm_ln_b, gm_w_s, gm_b_s, s5_lam_re, s5_lam_im, s5_log_dt, s5_b_re, s5_b_im, s5_c_re, s5_c_im, s5_d, s5_w_glu, s5_b_glu, w_out, ln1_g, ln1_b, w_up, conv_w, conv_b, w_down, ln2_g, ln2_b):
    depth, d, _ = w_in.shape
    batch = x.shape[0]
    gm_heads = gm_w_s.shape[1]
    gm = gm_heads * HEAD_DIM
    groups = s5_lam_re.shape[1]
    s5w = groups * S5_GROUP_DIM
    fox_heads = b_f.shape[1]
    fw = fox_heads * HEAD_DIM
    n_main = 2 * gm + s5w + 3 * fw
    alpha = (2.0 * depth) ** 0.25

    pad_rows = 2 * SUBLANES
    c_pad = jnp.pad(c, ((0, pad_rows - batch), (0, 0)))
    mod = _adaln(c_pad, w_ada, b_ada[:, None, :])[:, :batch]
    mod = mod.reshape(depth, batch, 6, d)

    n_state = groups * S5_STATE
    lre = s5_lam_re.reshape(depth, 1, n_state)
    lim = s5_lam_im.reshape(depth, 1, n_state)
    ldt = jnp.repeat(s5_log_dt, S5_STATE, axis=-1).reshape(depth, 1, n_state)
    bre_blk = _block_diag(jnp.swapaxes(s5_b_re, -1, -2))
    bim_blk = _block_diag(jnp.swapaxes(s5_b_im, -1, -2))
    cre_blk = _block_diag(jnp.swapaxes(s5_c_re, -1, -2))
    cim_blk = _block_diag(jnp.swapaxes(s5_c_im, -1, -2))
    lam, laml, bblk, cblk = _s5prep(lre, lim, ldt, bre_blk, bim_blk, cre_blk, cim_blk)
    wglu_blk = _block_diag(jnp.swapaxes(s5_w_glu, -1, -2)).astype(BF16)

    w_in_bf = w_in.astype(BF16)
    w_f = jnp.pad(w_in_bf[:, :, n_main:], ((0, 0), (0, 0), (0, LANES - fox_heads)))
    w_out_bf = w_out.astype(BF16)
    w_up_bf = w_up.astype(BF16)
    w_down_bf = w_down.astype(BF16)
    row3 = lambda a: a.reshape(depth, 1, -1)
    gm_g, gm_b = row3(gm_ln_g), row3(gm_ln_b)
    bs_full = jnp.repeat(jnp.swapaxes(gm_b_s, 1, 2), HEAD_DIM, axis=2)
    s5_d_row, s5_bglu_row = row3(s5_d), row3(s5_b_glu)
    bf_row = jnp.pad(b_f[:, None, :], ((0, 0), (0, 0), (0, LANES - fox_heads)))
    sel, ones = _fox_bias_tables(fox_heads)
    ln1_g3, ln1_b3, ln2_g3, ln2_b3 = row3(ln1_g), row3(ln1_b), row3(ln2_g), row3(ln2_b)
    conv_b3 = row3(conv_b)

    for l in range(depth):
        gm_u, gm_v, s5_in, q, k, v, fl = _inproj(x, mod, w_in_bf, w_f, l, gm, s5w, fw)
        y_gm = _gmlp(gm_u, gm_v, gm_g, gm_b, gm_w_s, bs_full, l)
        y_s5 = _s5(s5_in, lam, laml, bblk, cblk, l, s5_d_row, wglu_blk, s5_bglu_row)
        q_aug, k_aug = _foxprep(fl, q, k, bf_row, sel, ones, l, fox_heads)
        y_fox = _fox(q_aug, k_aug, v)
        x = _mixffn(y_gm, y_s5, y_fox, w_out_bf, x, mod, ln1_g3, ln1_b3,
                    w_up_bf, conv_w, conv_b3, w_down_bf, ln2_g3, ln2_b3, l, alpha)
    return x
```

```python
import functools
import math

import numpy as np
import jax
import jax.numpy as jnp
from jax import lax
from jax.experimental import pallas as pl
from jax.experimental.pallas import tpu as pltpu

F32 = jnp.float32
BF16 = jnp.bfloat16

HEAD_DIM = 64
GM_CHUNK = 128
S5_GROUP_DIM = 16
S5_STATE = 64
CONV_WIDTH = 3
LN_EPS = 1e-5
NEG_INF = -1e30
LOG2E = math.log2(math.e)

LANES = 128
SUBLANES = 8
ROW_TILE = 512
FFN_ROW_TILE = 512
S5_CHUNK = 128
S5_NCHUNK = SUBLANES
FOX_BLOCK = 512
FOX_HEADS_PER_STEP = 4
VMEM_LIMIT = 56 * 1024 * 1024


def _cparams(n_axes, vmem=VMEM_LIMIT, flags=None):
    return pltpu.CompilerParams(dimension_semantics=("arbitrary",) * n_axes,
                                vmem_limit_bytes=vmem, flags=flags)


def _layer_norm_rows(y, g, b):
    mu = jnp.mean(y, axis=-1, keepdims=True)
    d = y - mu
    var = jnp.mean(d * d, axis=-1, keepdims=True)
    return d * lax.rsqrt(var + LN_EPS) * g + b


def _gelu_tanh(x):
    c = math.sqrt(2.0 / math.pi)
    return 0.5 * x * (1.0 + jnp.tanh(c * (x + 0.044715 * (x * x * x))))


def _split3(x):
    hi = x.astype(BF16)
    r1 = x - hi.astype(F32)
    mid = r1.astype(BF16)
    lo = (r1 - mid.astype(F32)).astype(BF16)
    return hi, mid, lo


def _adaln_kernel(c_ref, w_ref, b_ref, o_ref):
    c = c_ref[...]
    cond = (c * jax.nn.sigmoid(c)).astype(BF16)
    w = w_ref[0].astype(BF16)
    o_ref[0] = jnp.dot(cond, w, preferred_element_type=F32) + b_ref[0]


def _adaln(c_pad, w_ada, b_ada3):
    depth, d, n = w_ada.shape
    rows = c_pad.shape[0]
    tn = 1536
    return pl.pallas_call(
        _adaln_kernel,
        grid=(depth, n // tn),
        in_specs=[pl.BlockSpec((rows, d), lambda l, j: (0, 0)),
                  pl.BlockSpec((1, d, tn), lambda l, j: (l, 0, j)),
                  pl.BlockSpec((1, 1, tn), lambda l, j: (l, 0, j))],
        out_specs=pl.BlockSpec((1, rows, tn), lambda l, j: (l, 0, j)),
        out_shape=jax.ShapeDtypeStruct((depth, rows, n), F32),
        compiler_params=_cparams(2),
        name="adaln",
    )(c_pad, w_ada, b_ada3)


def _layer_spec(a, layer, block=None):
    shape = tuple(a.shape[1:]) if block is None else tuple(block)
    return pl.BlockSpec((None,) + shape, lambda *_: (layer,) + (0,) * len(shape))


def _mod_spec(mod, layer):
    return pl.BlockSpec((None, 1) + tuple(mod.shape[2:]), lambda i, *_: (layer, i, 0, 0))


def _gmlp_tile(u_ref, v_ref, g_ref, b_ref, ws_ref, bs_ref, o_ref, heads):
    v = v_ref[...]
    rows, width = v.shape
    lane = lax.broadcasted_iota(jnp.int32, (rows, width), 1)
    head_masks = [(lane >= h * HEAD_DIM) & (lane < (h + 1) * HEAD_DIM) for h in range(heads)]

    def seg_mean(a):
        out = jnp.zeros_like(a)
        for m in head_masks:
            s = jnp.sum(jnp.where(m, a, 0.0), axis=-1, keepdims=True) * (1.0 / HEAD_DIM)
            out = jnp.where(m, s, out)
        return out

    d = v - seg_mean(v)
    var = seg_mean(d * d)
    vn = (d * lax.rsqrt(var + LN_EPS) * g_ref[...] + b_ref[...]).astype(BF16)

    r = lax.broadcasted_iota(jnp.int32, (GM_CHUNK, GM_CHUNK), 0)
    c = lax.broadcasted_iota(jnp.int32, (GM_CHUNK, GM_CHUNK), 1)
    tril = c <= r
    ws = [jnp.where(tril, ws_ref[h], 0.0).astype(BF16) for h in range(heads)]
    lane_c = lax.broadcasted_iota(jnp.int32, (GM_CHUNK, width), 1)
    chunk_masks = [(lane_c >= h * HEAD_DIM) & (lane_c < (h + 1) * HEAD_DIM) for h in range(heads)]
    bs = bs_ref[...]
    for j in range(rows // GM_CHUNK):
        r0 = j * GM_CHUNK
        vc = vn[r0:r0 + GM_CHUNK]
        z = jnp.zeros((GM_CHUNK, width), F32)
        for h in range(heads):
            zh = jnp.dot(ws[h], vc, preferred_element_type=F32)
            z = jnp.where(chunk_masks[h], zh, z)
        o_ref[0, r0:r0 + GM_CHUNK, :] = (u_ref[r0:r0 + GM_CHUNK, :] * (z + bs)).astype(BF16)


def _s5prep_kernel(lre_ref, lim_ref, ldt_ref, bre_ref, bim_ref, cre_ref, cim_ref,
                   lam_ref, laml_ref, bblk_ref, cblk_ref):
    lre = lre_ref[0]
    lim = lim_ref[0]
    dt = jnp.exp(ldt_ref[0])
    mag = jnp.exp(lre * dt)
    lbr = mag * jnp.cos(lim * dt)
    lbi = mag * jnp.sin(lim * dt)
    nr = lbr - 1.0
    den = lre * lre + lim * lim
    cr = (nr * lre + lbi * lim) / den
    ci = (lbi * lre - nr * lim) / den
    n = lre.shape[-1]
    lam_ref[0, :, 0:n] = lbr
    lam_ref[0, :, n:2 * n] = lbi
    pr, pi = lbr, lbi
    for _ in range(int(math.log2(S5_CHUNK))):
        pr, pi = pr * pr - pi * pi, 2.0 * (pr * pi)
    laml_ref[0, :, 0:n] = pr
    laml_ref[0, :, n:2 * n] = pi
    bre = bre_ref[0]
    bim = bim_ref[0]
    bblk_ref[0, :, 0:n] = (cr * bre - ci * bim).astype(BF16)
    bblk_ref[0, :, n:2 * n] = (cr * bim + ci * bre).astype(BF16)
    cblk_ref[0, 0:n, :] = cre_ref[0].astype(BF16)
    cblk_ref[0, n:2 * n, :] = (-cim_ref[0]).astype(BF16)


def _s5prep(lre, lim, ldt, bre_blk, bim_blk, cre_blk, cim_blk):
    depth, _, n = lre.shape
    w = bre_blk.shape[1]
    vec = pl.BlockSpec((1, 1, n), lambda l: (l, 0, 0))
    return pl.pallas_call(
        _s5prep_kernel,
        grid=(depth,),
        in_specs=[vec, vec, vec,
                  pl.BlockSpec((1, w, n), lambda l: (l, 0, 0)),
                  pl.BlockSpec((1, w, n), lambda l: (l, 0, 0)),
                  pl.BlockSpec((1, n, w), lambda l: (l, 0, 0)),
                  pl.BlockSpec((1, n, w), lambda l: (l, 0, 0))],
        out_specs=[pl.BlockSpec((1, 1, 2 * n), lambda l: (l, 0, 0)),
                   pl.BlockSpec((1, 1, 2 * n), lambda l: (l, 0, 0)),
                   pl.BlockSpec((1, w, 2 * n), lambda l: (l, 0, 0)),
                   pl.BlockSpec((1, 2 * n, w), lambda l: (l, 0, 0))],
        out_shape=[jax.ShapeDtypeStruct((depth, 1, 2 * n), F32),
                   jax.ShapeDtypeStruct((depth, 1, 2 * n), F32),
                   jax.ShapeDtypeStruct((depth, w, 2 * n), BF16),
                   jax.ShapeDtypeStruct((depth, 2 * n, w), BF16)],
        compiler_params=_cparams(1),
        name="s5prep",
    )(lre, lim, ldt, bre_blk, bim_blk, cre_blk, cim_blk)


def _s5_kernel(ua_ref, ub_ref, lam_ref, laml_ref, bblk_ref, cblk_ref, d_ref, wglu_ref, bglu_ref,
               o_ref, uperm, r_scr, xs_scr, e_scr, xin_scr, carry, y_scr, *, n):
    nc, lc = S5_NCHUNK, S5_CHUNK
    m_rows = nc * lc

    @pl.when(pl.program_id(1) == 0)
    def _():
        carry[...] = jnp.zeros_like(carry)

    def permute(t, _):
        r0 = pl.multiple_of(t * nc, nc)
        uperm[pl.ds(r0, nc), 0:LANES] = ua_ref[0, pl.ds(t, nc, stride=lc), :]
        uperm[pl.ds(r0, nc), LANES:2 * LANES] = ub_ref[0, pl.ds(t, nc, stride=lc), :]
        return 0
    lax.fori_loop(0, lc, permute, 0)

    up = uperm[...]
    r_scr[...] = jnp.dot(up.astype(BF16), bblk_ref[0], preferred_element_type=F32)

    lr = jnp.broadcast_to(lam_ref[0, :, 0:n], (nc, n))
    li = jnp.broadcast_to(lam_ref[0, :, n:2 * n], (nc, n))

    def scan(x0r, x0i, store):
        def body(t2, c):
            xr, xi = c
            r0 = pl.multiple_of(t2 * (2 * nc), 2 * nc)
            kept = []
            for half in range(2):
                br = r_scr[pl.ds(r0 + half * nc, nc), 0:n]
                bi = r_scr[pl.ds(r0 + half * nc, nc), n:2 * n]
                xr, xi = lr * xr - li * xi + br, lr * xi + li * xr + bi
                kept.append((xr, xi))
            if store:
                xs_scr[pl.ds(r0, 2 * nc), 0:n] = jnp.concatenate(
                    [kept[0][0], kept[1][0]], axis=0).astype(BF16)
                xs_scr[pl.ds(r0, 2 * nc), n:2 * n] = jnp.concatenate(
                    [kept[0][1], kept[1][1]], axis=0).astype(BF16)
            return xr, xi
        return lax.fori_loop(0, lc // 2, body, (x0r, x0i))

    zero = jnp.zeros((nc, n), F32)
    er, ei = scan(zero, zero, False)
    e_scr[:, 0:n] = er
    e_scr[:, n:2 * n] = ei

    llr = laml_ref[0, :, 0:n]
    lli = laml_ref[0, :, n:2 * n]
    cur_r = carry[:, 0:n]
    cur_i = carry[:, n:2 * n]
    for c in range(nc):
        xin_scr[c:c + 1, 0:n] = cur_r
        xin_scr[c:c + 1, n:2 * n] = cur_i
        e_r = e_scr[c:c + 1, 0:n]
        e_i = e_scr[c:c + 1, n:2 * n]
        cur_r, cur_i = llr * cur_r - lli * cur_i + e_r, llr * cur_i + lli * cur_r + e_i
    carry[:, 0:n] = cur_r
    carry[:, n:2 * n] = cur_i

    scan(xin_scr[:, 0:n], xin_scr[:, n:2 * n], True)

    step = 256
    for r0 in range(0, m_rows, step):
        y = jnp.dot(xs_scr[r0:r0 + step, :], cblk_ref[0], preferred_element_type=F32)
        y = y + d_ref[...] * uperm[r0:r0 + step, :]
        y = _gelu_tanh(y)
        gate = jnp.dot(y.astype(BF16), wglu_ref[...], preferred_element_type=F32) + bglu_ref[...]
        out = y * jax.nn.sigmoid(gate)
        for hf in range(out.shape[1] // LANES):
            y_scr[hf, r0:r0 + step, :] = out[:, hf * LANES:(hf + 1) * LANES]

    for c in range(nc):
        for hf in range(y_scr.shape[0]):
            o_ref[0, c * lc:(c + 1) * lc, hf * LANES:(hf + 1) * LANES] = (
                y_scr[hf, pl.ds(c, lc, stride=nc), :].astype(BF16))


def _s5(u, lam, laml, bblk, cblk, layer, d_row, wglu_blk, bglu_row):
    b, s, w = u.shape
    n = lam.shape[-1] // 2
    m_rows = S5_NCHUNK * S5_CHUNK
    per_layer = lambda a: pl.BlockSpec((1,) + a.shape[1:], lambda i, j: (layer,) + (0,) * (a.ndim - 1))
    return pl.pallas_call(
        functools.partial(_s5_kernel, n=n),
        grid=(b, s // m_rows),
        in_specs=[pl.BlockSpec((1, m_rows, LANES), lambda i, j: (i, j, 0)),
                  pl.BlockSpec((1, m_rows, LANES), lambda i, j: (i, j, 1)),
                  per_layer(lam), per_layer(laml), per_layer(bblk), per_layer(cblk),
                  _layer_spec(d_row, layer), _layer_spec(wglu_blk, layer),
                  _layer_spec(bglu_row, layer)],
        out_specs=pl.BlockSpec((1, m_rows, w), lambda i, j: (i, j, 0)),
        out_shape=jax.ShapeDtypeStruct((b, s, w), BF16),
        scratch_shapes=[pltpu.VMEM((m_rows, w), F32),
                        pltpu.VMEM((m_rows, 2 * n), F32),
                        pltpu.VMEM((m_rows, 2 * n), BF16),
                        pltpu.VMEM((S5_NCHUNK, 2 * n), F32),
                        pltpu.VMEM((S5_NCHUNK, 2 * n), F32),
                        pltpu.VMEM((1, 2 * n), F32),
                        pltpu.VMEM((w // LANES, m_rows, LANES), F32)],
        compiler_params=_cparams(2),
        name="s5",
    )(u, u, lam, laml, bblk, cblk, d_row, wglu_blk, bglu_row)


def _fox_bias_tables(heads):
    sel = np.zeros((LANES, 2 * heads * LANES), np.float32)
    ones = np.zeros((1, 2 * heads * LANES), np.float32)
    koff = heads * LANES
    for h in range(heads):
        base = h * LANES + (HEAD_DIM if h % 2 == 0 else 0)
        for part in range(3):
            sel[part * heads + h, base + part] = 1.0
            ones[0, base + 3 + part] = 1.0
            ones[0, koff + base + part] = 1.0
            sel[part * heads + h, koff + base + 3 + part] = -1.0
    return jnp.asarray(sel, BF16), jnp.asarray(ones, F32)


def _foxprep_tile(fl, q_ref, k_ref, bf_ref, sel_ref, ones_ref, qa_ref, ka_ref, carry, cum_scr, heads):
    rows = fl.shape[0]
    x = fl + bf_ref[...]
    lane = lax.broadcasted_iota(jnp.int32, (rows, LANES), 1)
    ls = jnp.minimum(x, 0.0) - jnp.log1p(jnp.exp(-jnp.abs(x)))
    ls = jnp.where(lane < heads, ls, 0.0)

    r = lax.broadcasted_iota(jnp.int32, (LANES, LANES), 0)
    c = lax.broadcasted_iota(jnp.int32, (LANES, LANES), 1)
    ltri = jnp.where(c <= r, 1.0, 0.0).astype(BF16)
    run = carry[...]
    for j in range(rows // LANES):
        blk = ls[j * LANES:(j + 1) * LANES]
        parts = jnp.dot(ltri, jnp.concatenate(_split3(blk), axis=1), preferred_element_type=F32)
        loc = parts[:, 0:LANES] + parts[:, LANES:2 * LANES] + parts[:, 2 * LANES:3 * LANES]
        cum_blk = loc + run
        run = cum_blk[LANES - 1:LANES, :]
        cum_scr[j * LANES:(j + 1) * LANES, :] = cum_blk
    carry[...] = run

    chi, cmid, clo = _split3(cum_scr[...] * LOG2E)
    c24 = (chi.astype(F32) + pltpu.roll(cmid.astype(F32), heads, 1)
           + pltpu.roll(clo.astype(F32), 2 * heads, 1)).astype(BF16)
    bias = jnp.dot(c24, sel_ref[...], preferred_element_type=F32) + ones_ref[...]
    bias = bias.astype(BF16)
    koff = heads * LANES
    for h in range(heads):
        pair = (h // 2) * LANES
        keep = (lane < HEAD_DIM) if h % 2 == 0 else (lane >= HEAD_DIM)
        qa_ref[0, h] = jnp.where(keep, q_ref[:, pair:pair + LANES],
                                 bias[:, h * LANES:(h + 1) * LANES])
        ka_ref[0, h] = jnp.where(keep, k_ref[:, pair:pair + LANES],
                                 bias[:, koff + h * LANES:koff + (h + 1) * LANES])


def _premix_kernel(x_ref, mod_ref, w_ref, wf_ref, lng_ref, lnb_ref, ws_ref, bs_ref,
                   bf_ref, sel_ref, ones_ref,
                   ygm_ref, s5_ref, v_ref, qa_ref, ka_ref,
                   h_scr, gu_scr, gv_scr, q_scr, k_scr, carry, cum_scr,
                   *, gm, s5w, fw, gm_heads, fox_heads):
    @pl.when(pl.program_id(1) == 0)
    def _():
        carry[...] = jnp.zeros_like(carry)

    sh = mod_ref[0, 0:1, :]
    sc = mod_ref[0, 1:2, :]
    h_scr[...] = (x_ref[0] * (1.0 + sc) + sh).astype(BF16)
    h = h_scr[...]

    def proj(lo, hi):
        return jnp.dot(h, w_ref[:, lo:hi], preferred_element_type=F32)

    o = 0
    gu_scr[...] = proj(o, o + gm); o += gm
    gv_scr[...] = proj(o, o + gm); o += gm
    s5_ref[0] = proj(o, o + s5w); o += s5w
    q_scr[...] = (proj(o, o + fw) * (LOG2E * HEAD_DIM ** -0.5)).astype(BF16); o += fw
    k_scr[...] = proj(o, o + fw).astype(BF16); o += fw
    v_ref[0] = proj(o, o + fw).astype(BF16)
    fl = jnp.dot(h, wf_ref[...], preferred_element_type=F32)

    _gmlp_tile(gu_scr, gv_scr, lng_ref, lnb_ref, ws_ref, bs_ref, ygm_ref, gm_heads)
    _foxprep_tile(fl, q_scr, k_scr, bf_ref, sel_ref, ones_ref, qa_ref, ka_ref, carry, cum_scr,
                  fox_heads)


def _premix(x, mod, w_main, w_f, ln_g_row, ln_b_row, w_s, b_s_full, bf_row, sel, ones, layer,
            gm, s5w, fw):
    b, s, d = x.shape
    tm = ROW_TILE
    gm_heads = w_s.shape[1]
    fox_heads = fw // HEAD_DIM
    row = lambda n: pl.BlockSpec((1, tm, n), lambda i, j: (i, j, 0))
    full = lambda a: pl.BlockSpec(a.shape, lambda i, j: (0,) * a.ndim)
    aug = pl.BlockSpec((1, fox_heads, tm, LANES), lambda i, j: (i, 0, j, 0))
    lay = lambda a: _layer_spec(a, layer)
    outs = [jax.ShapeDtypeStruct((b, s, gm), BF16), jax.ShapeDtypeStruct((b, s, s5w), F32),
            jax.ShapeDtypeStruct((b, s, fw), BF16),
            jax.ShapeDtypeStruct((b, fox_heads, s, LANES), BF16),
            jax.ShapeDtypeStruct((b, fox_heads, s, LANES), BF16)]
    return pl.pallas_call(
        functools.partial(_premix_kernel, gm=gm, s5w=s5w, fw=fw, gm_heads=gm_heads,
                          fox_heads=fox_heads),
        grid=(b, s // tm),
        in_specs=[row(d), _mod_spec(mod, layer), lay(w_main), lay(w_f), lay(ln_g_row),
                  lay(ln_b_row), lay(w_s), lay(b_s_full), lay(bf_row), full(sel), full(ones)],
        out_specs=[row(gm), row(s5w), row(fw), aug, aug],
        out_shape=outs,
        scratch_shapes=[pltpu.VMEM((tm, d), BF16),
                        pltpu.VMEM((tm, gm), F32), pltpu.VMEM((tm, gm), F32),
                        pltpu.VMEM((tm, fw), BF16), pltpu.VMEM((tm, fw), BF16),
                        pltpu.VMEM((1, LANES), F32), pltpu.VMEM((tm, LANES), F32)],
        compiler_params=_cparams(2),
        name="premix",
    )(x, mod, w_main, w_f, ln_g_row, ln_b_row, w_s, b_s_full, bf_row, sel, ones)


def _fox_kernel(qa_ref, ka_ref, v_ref, o_ref, va_scr):
    qi = pl.program_id(2)
    nh = qa_ref.shape[1]
    tq = qa_ref.shape[2]
    tk = tq
    nt = (((1,), (1,)), ((), ()))

    @pl.when(qi == 0)
    def _():
        lane_v = lax.broadcasted_iota(jnp.int32, va_scr.shape[1:], 1)
        one = jnp.ones((), BF16)
        for hp in range(nh // 2):
            v2 = v_ref[0, :, hp * LANES:(hp + 1) * LANES]
            va_scr[2 * hp] = jnp.where(lane_v < HEAD_DIM, v2, one)
            va_scr[2 * hp + 1] = jnp.where(lane_v >= HEAD_DIM, v2, one)

    def step(k0, width, state, mask_off):
        logits = [lax.dot_general(qa_ref[0, hh], ka_ref[0, hh, pl.ds(k0, width), :], nt,
                                  preferred_element_type=F32) for hh in range(nh)]
        new = []
        for hh in range(nh):
            m, acc = state[hh]
            s = logits[hh]
            if mask_off is not None:
                row = lax.broadcasted_iota(jnp.int32, (tq, width), 0)
                col = lax.broadcasted_iota(jnp.int32, (tq, width), 1)
                s = jnp.where(col <= row + mask_off, s, NEG_INF)
            m_new = jnp.maximum(m, jnp.max(s, axis=-1, keepdims=True))
            alpha = jnp.exp2(m - m_new)
            p = jnp.exp2(s - m_new)
            pv = jnp.dot(p.astype(BF16), va_scr[hh, pl.ds(k0, width), :],
                         preferred_element_type=F32)
            new.append((m_new, alpha * acc + pv))
        return tuple(new)

    wide = 2 * tk
    n_wide = qi // 2
    init = tuple((jnp.full((tq, 1), -jnp.inf, F32), jnp.zeros((tq, LANES), F32))
                 for _ in range(nh))
    state = lax.fori_loop(
        0, n_wide, lambda j, st: step(pl.multiple_of(j * wide, wide), wide, st, None), init)
    k_tail = pl.multiple_of(n_wide * wide, wide)
    state = lax.cond(qi % 2 == 1,
                     lambda st: step(k_tail, wide, st, tk),
                     lambda st: step(k_tail, tk, st, 0),
                     state)
    outs = [acc / pltpu.roll(acc, HEAD_DIM, 1) for _, acc in state]
    lane = lax.broadcasted_iota(jnp.int32, (tq, LANES), 1)
    for hp in range(nh // 2):
        o_ref[0, :, hp * LANES:(hp + 1) * LANES] = jnp.where(
            lane < HEAD_DIM, outs[2 * hp], outs[2 * hp + 1]).astype(BF16)


def _fox(q_aug, k_aug, v):
    b, heads, s, _ = q_aug.shape
    fw = v.shape[-1]
    tq = FOX_BLOCK
    nh = FOX_HEADS_PER_STEP
    vw = nh // 2 * LANES
    return pl.pallas_call(
        _fox_kernel,
        grid=(b, heads // nh, s // tq),
        in_specs=[pl.BlockSpec((1, nh, tq, LANES), lambda i, p, j: (i, p, j, 0)),
                  pl.BlockSpec((1, nh, s, LANES), lambda i, p, j: (i, p, 0, 0)),
                  pl.BlockSpec((1, s, vw), lambda i, p, j: (i, 0, p))],
        out_specs=pl.BlockSpec((1, tq, vw), lambda i, p, j: (i, j, p)),
        out_shape=jax.ShapeDtypeStruct((b, s, fw), BF16),
        scratch_shapes=[pltpu.VMEM((nh, s, LANES), BF16)],
        compiler_params=_cparams(3),
        name="fox",
    )(q_aug, k_aug, v)


def _mixffn_kernel(ygm_ref, ys5_ref, yfox_ref, wout_ref, x_ref, mod_ref, g1_ref, b1_ref,
                   wup_ref, cw_ref, cb_ref, wdn_ref, g2_ref, b2_ref, o_ref,
                   x1_scr, h_scr, halo, act_scr, *, alpha, gm, s5w, dff, chunk):
    @pl.when(pl.program_id(1) == 0)
    def _():
        halo[...] = jnp.zeros_like(halo)

    mix = jnp.dot(ygm_ref[0], wout_ref[0:gm, :], preferred_element_type=F32)
    mix += jnp.dot(ys5_ref[0], wout_ref[gm:gm + s5w, :], preferred_element_type=F32)
    mix += jnp.dot(yfox_ref[0], wout_ref[gm + s5w:, :], preferred_element_type=F32)
    gate1 = mod_ref[0, 2:3, :]
    x1_scr[...] = _layer_norm_rows(alpha * x_ref[0] + (1.0 + gate1) * mix, g1_ref[...], b1_ref[...])

    rows = x_ref.shape[1]
    sh = mod_ref[0, 3:4, :]
    sc = mod_ref[0, 4:5, :]
    gate = mod_ref[0, 5:6, :]
    h_scr[...] = (x1_scr[...] * (1.0 + sc) + sh).astype(BF16)
    h = h_scr[...]
    rid = lax.broadcasted_iota(jnp.int32, (rows, chunk), 0)
    for ci in range(dff // chunk):
        c0 = ci * chunk
        a = jnp.dot(h, wup_ref[:, c0:c0 + chunk], preferred_element_type=F32)
        gt = jnp.dot(h, wup_ref[:, dff + c0:dff + c0 + chunk], preferred_element_type=F32)
        p1 = halo[SUBLANES - 1:SUBLANES, c0:c0 + chunk]
        p2 = halo[SUBLANES - 2:SUBLANES - 1, c0:c0 + chunk]
        a1 = jnp.where(rid == 0, p1, pltpu.roll(a, 1, 0))
        a2 = jnp.where(rid == 0, p2, jnp.where(rid == 1, p1, pltpu.roll(a, 2, 0)))
        halo[:, c0:c0 + chunk] = a[rows - SUBLANES:rows, :]
        conv = (cb_ref[:, c0:c0 + chunk] + cw_ref[0:1, c0:c0 + chunk] * a2
                + cw_ref[1:2, c0:c0 + chunk] * a1 + cw_ref[2:3, c0:c0 + chunk] * a)
        act_scr[:, c0:c0 + chunk] = (_gelu_tanh(conv) * gt).astype(BF16)
    ffn = jnp.dot(act_scr[...], wdn_ref[...], preferred_element_type=F32)
    y = alpha * x1_scr[...] + (1.0 + gate) * ffn
    o_ref[0] = _layer_norm_rows(y, g2_ref[...], b2_ref[...])


def _mixffn(y_gm, y_s5, y_fox, w_out, x, mod, g1_row, b1_row,
            w_up, conv_w, conv_b_row, w_down, g2_row, b2_row, layer, alpha):
    b, s, d = x.shape
    gm, s5w, fw = y_gm.shape[-1], y_s5.shape[-1], y_fox.shape[-1]
    dff = w_down.shape[1]
    tm = FFN_ROW_TILE
    chunk = 256
    row = lambda n: pl.BlockSpec((1, tm, n), lambda i, j: (i, j, 0))
    full = lambda a: _layer_spec(a, layer)
    resident = lambda a: pl.BlockSpec((None,) + a.shape[1:], lambda i, j: (layer,) + (0,) * (a.ndim - 1),
                                      pipeline_mode=pl.Buffered(1))
    return pl.pallas_call(
        functools.partial(_mixffn_kernel, alpha=alpha, gm=gm, s5w=s5w, dff=dff, chunk=chunk),
        grid=(b, s // tm),
        in_specs=[row(gm), row(s5w), row(fw), resident(w_out), row(d),
                  _mod_spec(mod, layer), full(g1_row), full(b1_row),
                  resident(w_up), full(conv_w), full(conv_b_row), resident(w_down),
                  full(g2_row), full(b2_row)],
        out_specs=row(d),
        out_shape=jax.ShapeDtypeStruct((b, s, d), F32),
        scratch_shapes=[pltpu.VMEM((tm, d), F32),
                        pltpu.VMEM((tm, d), BF16),
                        pltpu.VMEM((SUBLANES, dff), F32),
                        pltpu.VMEM((tm, dff), BF16)],
        compiler_params=_cparams(2),
        name="mixffn",
    )(y_gm, y_s5, y_fox, w_out, x, mod, g1_row, b1_row,
      w_up, conv_w, conv_b_row, w_down, g2_row, b2_row)


def _block_diag(a):
    depth, g, r, c = a.shape
    eye = jnp.eye(g, dtype=a.dtype)
    return (a[:, :, :, None, :] * eye[None, :, None, :, None]).reshape(depth, g * r, g * c)


def kernel(x, c, w_ada, b_ada, w_in, b_f, gm_ln_g, gm_ln_b, gm_w_s, gm_b_s, s5_lam_re, s5_lam_im, s5_log_dt, s5_b_re, s5_b_im, s5_c_re, s5_c_im, s5_d, s5_w_glu, s5_b_glu, w_out, ln1_g, ln1_b, w_up, conv_w, conv_b, w_down, ln2_g, ln2_b):
    depth, d, _ = w_in.shape
    batch = x.shape[0]
    gm_heads = gm_w_s.shape[1]
    gm = gm_heads * HEAD_DIM
    groups = s5_lam_re.shape[1]
    s5w = groups * S5_GROUP_DIM
    fox_heads = b_f.shape[1]
    fw = fox_heads * HEAD_DIM
    n_main = 2 * gm + s5w + 3 * fw
    alpha = (2.0 * depth) ** 0.25

    pad_rows = 2 * SUBLANES
    c_pad = jnp.pad(c, ((0, pad_rows - batch), (0, 0)))
    mod = _adaln(c_pad, w_ada, b_ada[:, None, :])[:, :batch]
    mod = mod.reshape(depth, batch, 6, d)

    n_state = groups * S5_STATE
    lre = s5_lam_re.reshape(depth, 1, n_state)
    lim = s5_lam_im.reshape(depth, 1, n_state)
    ldt = jnp.repeat(s5_log_dt, S5_STATE, axis=-1).reshape(depth, 1, n_state)
    bre_blk = _block_diag(jnp.swapaxes(s5_b_re, -1, -2))
    bim_blk = _block_diag(jnp.swapaxes(s5_b_im, -1, -2))
    cre_blk = _block_diag(jnp.swapaxes(s5_c_re, -1, -2))
    cim_blk = _block_diag(jnp.swapaxes(s5_c_im, -1, -2))
    lam, laml, bblk, cblk = _s5prep(lre, lim, ldt, bre_blk, bim_blk, cre_blk, cim_blk)
    wglu_blk = _block_diag(jnp.swapaxes(s5_w_glu, -1, -2)).astype(BF16)

    w_main = w_in[:, :, :n_main].astype(BF16)
    w_f = jnp.pad(w_in[:, :, n_main:].astype(BF16), ((0, 0), (0, 0), (0, LANES - fox_heads)))
    w_out_bf = w_out.astype(BF16)
    w_up_bf = w_up.astype(BF16)
    w_down_bf = w_down.astype(BF16)
    row3 = lambda a: a.reshape(depth, 1, -1)
    gm_g, gm_b = row3(gm_ln_g), row3(gm_ln_b)
    bs_full = jnp.repeat(jnp.swapaxes(gm_b_s, 1, 2), HEAD_DIM, axis=2)
    s5_d_row, s5_bglu_row = row3(s5_d), row3(s5_b_glu)
    bf_row = jnp.pad(b_f[:, None, :], ((0, 0), (0, 0), (0, LANES - fox_heads)))
    sel, ones = _fox_bias_tables(fox_heads)
    ln1_g3, ln1_b3, ln2_g3, ln2_b3 = row3(ln1_g), row3(ln1_b), row3(ln2_g), row3(ln2_b)
    conv_b3 = row3(conv_b)

    for l in range(depth):
        y_gm, s5_in, v, q_aug, k_aug = _premix(x, mod, w_main, w_f, gm_g, gm_b, gm_w_s, bs_full,
                                               bf_row, sel, ones, l, gm, s5w, fw)
        y_s5 = _s5(s5_in, lam, laml, bblk, cblk, l, s5_d_row, wglu_blk, s5_bglu_row)
        y_fox = _fox(q_aug, k_aug, v)
        x = _mixffn(y_gm, y_s5, y_fox, w_out_bf, x, mod, ln1_g3, ln1_b3,
                    w_up_bf, conv_w, conv_b3, w_down_bf, ln2_g3, ln2_b3, l, alpha)
    return x
```

```python
import functools
import math

import numpy as np
import jax
import jax.numpy as jnp
from jax import lax
from jax.experimental import pallas as pl
from jax.experimental.pallas import tpu as pltpu

F32 = jnp.float32
BF16 = jnp.bfloat16

HEAD_DIM = 64
GM_CHUNK = 128
S5_GROUP_DIM = 16
S5_STATE = 64
CONV_WIDTH = 3
LN_EPS = 1e-5
NEG_INF = -1e30
LOG2E = math.log2(math.e)

LANES = 128
SUBLANES = 8
ROW_TILE = 512
FFN_ROW_TILE = 512
S5_CHUNK = 128
S5_NCHUNK = SUBLANES
FOX_BLOCK = 512
FOX_HEADS_PER_STEP = 4
VMEM_LIMIT = 56 * 1024 * 1024


def _cparams(n_axes, vmem=VMEM_LIMIT, flags=None):
    return pltpu.CompilerParams(dimension_semantics=("arbitrary",) * n_axes,
                                vmem_limit_bytes=vmem, flags=flags)


def _layer_norm_rows(y, g, b):
    mu = jnp.mean(y, axis=-1, keepdims=True)
    d = y - mu
    var = jnp.mean(d * d, axis=-1, keepdims=True)
    return d * lax.rsqrt(var + LN_EPS) * g + b


def _gelu_tanh(x):
    c = math.sqrt(2.0 / math.pi)
    return 0.5 * x * (1.0 + jnp.tanh(c * (x + 0.044715 * (x * x * x))))


def _split3(x):
    hi = x.astype(BF16)
    r1 = x - hi.astype(F32)
    mid = r1.astype(BF16)
    lo = (r1 - mid.astype(F32)).astype(BF16)
    return hi, mid, lo


def _adaln_kernel(c_ref, w_ref, b_ref, o_ref):
    c = c_ref[...]
    cond = (c * jax.nn.sigmoid(c)).astype(BF16)
    w = w_ref[0].astype(BF16)
    o_ref[0] = jnp.dot(cond, w, preferred_element_type=F32) + b_ref[0]


def _adaln(c_pad, w_ada, b_ada3):
    depth, d, n = w_ada.shape
    rows = c_pad.shape[0]
    tn = 1536
    return pl.pallas_call(
        _adaln_kernel,
        grid=(depth, n // tn),
        in_specs=[pl.BlockSpec((rows, d), lambda l, j: (0, 0)),
                  pl.BlockSpec((1, d, tn), lambda l, j: (l, 0, j)),
                  pl.BlockSpec((1, 1, tn), lambda l, j: (l, 0, j))],
        out_specs=pl.BlockSpec((1, rows, tn), lambda l, j: (l, 0, j)),
        out_shape=jax.ShapeDtypeStruct((depth, rows, n), F32),
        compiler_params=_cparams(2),
        name="adaln",
    )(c_pad, w_ada, b_ada3)


def _layer_spec(a, layer, block=None):
    shape = tuple(a.shape[1:]) if block is None else tuple(block)
    return pl.BlockSpec((None,) + shape, lambda *_: (layer,) + (0,) * len(shape))


def _mod_spec(mod, layer):
    return pl.BlockSpec((None, 1) + tuple(mod.shape[2:]), lambda i, *_: (layer, i, 0, 0))


def _gmlp_tile(u_ref, v_ref, g_ref, b_ref, ws_ref, bs_ref, o_ref, heads):
    v = v_ref[...]
    rows, width = v.shape
    lane = lax.broadcasted_iota(jnp.int32, (rows, width), 1)
    head_masks = [(lane >= h * HEAD_DIM) & (lane < (h + 1) * HEAD_DIM) for h in range(heads)]

    def seg_mean(a):
        out = jnp.zeros_like(a)
        for m in head_masks:
            s = jnp.sum(jnp.where(m, a, 0.0), axis=-1, keepdims=True) * (1.0 / HEAD_DIM)
            out = jnp.where(m, s, out)
        return out

    d = v - seg_mean(v)
    var = seg_mean(d * d)
    vn = (d * lax.rsqrt(var + LN_EPS) * g_ref[...] + b_ref[...]).astype(BF16)

    r = lax.broadcasted_iota(jnp.int32, (GM_CHUNK, GM_CHUNK), 0)
    c = lax.broadcasted_iota(jnp.int32, (GM_CHUNK, GM_CHUNK), 1)
    tril = c <= r
    ws = [jnp.where(tril, ws_ref[h], 0.0).astype(BF16) for h in range(heads)]
    lane_c = lax.broadcasted_iota(jnp.int32, (GM_CHUNK, width), 1)
    chunk_masks = [(lane_c >= h * HEAD_DIM) & (lane_c < (h + 1) * HEAD_DIM) for h in range(heads)]
    bs = bs_ref[...]
    for j in range(rows // GM_CHUNK):
        r0 = j * GM_CHUNK
        vc = vn[r0:r0 + GM_CHUNK]
        z = jnp.zeros((GM_CHUNK, width), F32)
        for h in range(heads):
            zh = jnp.dot(ws[h], vc, preferred_element_type=F32)
            z = jnp.where(chunk_masks[h], zh, z)
        o_ref[0, r0:r0 + GM_CHUNK, :] = (u_ref[r0:r0 + GM_CHUNK, :] * (z + bs)).astype(BF16)


def _s5prep_kernel(lre_ref, lim_ref, ldt_ref, bre_ref, bim_ref, cre_ref, cim_ref,
                   lam_ref, laml_ref, bblk_ref, cblk_ref):
    lre = lre_ref[0]
    lim = lim_ref[0]
    dt = jnp.exp(ldt_ref[0])
    mag = jnp.exp(lre * dt)
    lbr = mag * jnp.cos(lim * dt)
    lbi = mag * jnp.sin(lim * dt)
    nr = lbr - 1.0
    den = lre * lre + lim * lim
    cr = (nr * lre + lbi * lim) / den
    ci = (lbi * lre - nr * lim) / den
    n = lre.shape[-1]
    lam_ref[0, :, 0:n] = lbr
    lam_ref[0, :, n:2 * n] = lbi
    pr, pi = lbr, lbi
    for _ in range(int(math.log2(S5_CHUNK))):
        pr, pi = pr * pr - pi * pi, 2.0 * (pr * pi)
    laml_ref[0, :, 0:n] = pr
    laml_ref[0, :, n:2 * n] = pi
    bre = bre_ref[0]
    bim = bim_ref[0]
    bblk_ref[0, :, 0:n] = (cr * bre - ci * bim).astype(BF16)
    bblk_ref[0, :, n:2 * n] = (cr * bim + ci * bre).astype(BF16)
    cblk_ref[0, 0:n, :] = cre_ref[0].astype(BF16)
    cblk_ref[0, n:2 * n, :] = (-cim_ref[0]).astype(BF16)


def _s5prep(lre, lim, ldt, bre_blk, bim_blk, cre_blk, cim_blk):
    depth, _, n = lre.shape
    w = bre_blk.shape[1]
    vec = pl.BlockSpec((1, 1, n), lambda l: (l, 0, 0))
    return pl.pallas_call(
        _s5prep_kernel,
        grid=(depth,),
        in_specs=[vec, vec, vec,
                  pl.BlockSpec((1, w, n), lambda l: (l, 0, 0)),
                  pl.BlockSpec((1, w, n), lambda l: (l, 0, 0)),
                  pl.BlockSpec((1, n, w), lambda l: (l, 0, 0)),
                  pl.BlockSpec((1, n, w), lambda l: (l, 0, 0))],
        out_specs=[pl.BlockSpec((1, 1, 2 * n), lambda l: (l, 0, 0)),
                   pl.BlockSpec((1, 1, 2 * n), lambda l: (l, 0, 0)),
                   pl.BlockSpec((1, w, 2 * n), lambda l: (l, 0, 0)),
                   pl.BlockSpec((1, 2 * n, w), lambda l: (l, 0, 0))],
        out_shape=[jax.ShapeDtypeStruct((depth, 1, 2 * n), F32),
                   jax.ShapeDtypeStruct((depth, 1, 2 * n), F32),
                   jax.ShapeDtypeStruct((depth, w, 2 * n), BF16),
                   jax.ShapeDtypeStruct((depth, 2 * n, w), BF16)],
        compiler_params=_cparams(1),
        name="s5prep",
    )(lre, lim, ldt, bre_blk, bim_blk, cre_blk, cim_blk)


def _s5_kernel(ua_ref, ub_ref, lam_ref, laml_ref, bblk_ref, cblk_ref, d_ref, wglu_ref, bglu_ref,
               o_ref, r_scr, xs_scr, e_scr, xin_scr, carry, y_scr, *, n):
    nc, lc = S5_NCHUNK, S5_CHUNK
    m_rows = nc * lc

    @pl.when(pl.program_id(1) == 0)
    def _():
        carry[...] = jnp.zeros_like(carry)

    u_rows = lambda r0, r1: jnp.concatenate([ua_ref[0, r0:r1, :], ub_ref[0, r0:r1, :]], axis=1)
    r_scr[...] = jnp.dot(u_rows(0, m_rows).astype(BF16), bblk_ref[0],
                         preferred_element_type=F32)

    lr = jnp.broadcast_to(lam_ref[0, :, 0:n], (nc, n))
    li = jnp.broadcast_to(lam_ref[0, :, n:2 * n], (nc, n))

    def scan(x0r, x0i, store):
        def body(t2, c):
            xr, xi = c
            r0 = pl.multiple_of(t2 * (2 * nc), 2 * nc)
            kept = []
            for half in range(2):
                br = r_scr[pl.ds(r0 + half * nc, nc), 0:n]
                bi = r_scr[pl.ds(r0 + half * nc, nc), n:2 * n]
                xr, xi = lr * xr - li * xi + br, lr * xi + li * xr + bi
                kept.append((xr, xi))
            if store:
                xs_scr[pl.ds(r0, 2 * nc), 0:n] = jnp.concatenate(
                    [kept[0][0], kept[1][0]], axis=0).astype(BF16)
                xs_scr[pl.ds(r0, 2 * nc), n:2 * n] = jnp.concatenate(
                    [kept[0][1], kept[1][1]], axis=0).astype(BF16)
            return xr, xi
        return lax.fori_loop(0, lc // 2, body, (x0r, x0i))

    zero = jnp.zeros((nc, n), F32)
    er, ei = scan(zero, zero, False)
    e_scr[:, 0:n] = er
    e_scr[:, n:2 * n] = ei

    llr = laml_ref[0, :, 0:n]
    lli = laml_ref[0, :, n:2 * n]
    cur_r = carry[:, 0:n]
    cur_i = carry[:, n:2 * n]
    for c in range(nc):
        xin_scr[c:c + 1, 0:n] = cur_r
        xin_scr[c:c + 1, n:2 * n] = cur_i
        e_r = e_scr[c:c + 1, 0:n]
        e_i = e_scr[c:c + 1, n:2 * n]
        cur_r, cur_i = llr * cur_r - lli * cur_i + e_r, llr * cur_i + lli * cur_r + e_i
    carry[:, 0:n] = cur_r
    carry[:, n:2 * n] = cur_i

    scan(xin_scr[:, 0:n], xin_scr[:, n:2 * n], True)

    step = 256
    for r0 in range(0, m_rows, step):
        y = jnp.dot(xs_scr[r0:r0 + step, :], cblk_ref[0], preferred_element_type=F32)
        y = y + d_ref[...] * u_rows(r0, r0 + step)
        y = _gelu_tanh(y)
        gate = jnp.dot(y.astype(BF16), wglu_ref[...], preferred_element_type=F32) + bglu_ref[...]
        out = y * jax.nn.sigmoid(gate)
        for hf in range(out.shape[1] // LANES):
            y_scr[hf, r0:r0 + step, :] = out[:, hf * LANES:(hf + 1) * LANES]

    for c in range(nc):
        for hf in range(y_scr.shape[0]):
            o_ref[0, c * lc:(c + 1) * lc, hf * LANES:(hf + 1) * LANES] = (
                y_scr[hf, pl.ds(c, lc, stride=nc), :].astype(BF16))


def _s5(u_a, u_b, lam, laml, bblk, cblk, layer, d_row, wglu_blk, bglu_row):
    b, s, _ = u_a.shape
    w = 2 * LANES
    n = lam.shape[-1] // 2
    m_rows = S5_NCHUNK * S5_CHUNK
    per_layer = lambda a: pl.BlockSpec((1,) + a.shape[1:], lambda i, j: (layer,) + (0,) * (a.ndim - 1))
    half = pl.BlockSpec((1, m_rows, LANES), lambda i, j: (i, j, 0))
    return pl.pallas_call(
        functools.partial(_s5_kernel, n=n),
        grid=(b, s // m_rows),
        in_specs=[half, half,
                  per_layer(lam), per_layer(laml), per_layer(bblk), per_layer(cblk),
                  _layer_spec(d_row, layer), _layer_spec(wglu_blk, layer),
                  _layer_spec(bglu_row, layer)],
        out_specs=pl.BlockSpec((1, m_rows, w), lambda i, j: (i, j, 0)),
        out_shape=jax.ShapeDtypeStruct((b, s, w), BF16),
        scratch_shapes=[pltpu.VMEM((m_rows, 2 * n), F32),
                        pltpu.VMEM((m_rows, 2 * n), BF16),
                        pltpu.VMEM((S5_NCHUNK, 2 * n), F32),
                        pltpu.VMEM((S5_NCHUNK, 2 * n), F32),
                        pltpu.VMEM((1, 2 * n), F32),
                        pltpu.VMEM((w // LANES, m_rows, LANES), F32)],
        compiler_params=_cparams(2),
        name="s5",
    )(u_a, u_b, lam, laml, bblk, cblk, d_row, wglu_blk, bglu_row)


def _fox_bias_tables(heads):
    sel = np.zeros((LANES, 2 * heads * LANES), np.float32)
    ones = np.zeros((1, 2 * heads * LANES), np.float32)
    koff = heads * LANES
    for h in range(heads):
        base = h * LANES + (HEAD_DIM if h % 2 == 0 else 0)
        for part in range(3):
            sel[part * heads + h, base + part] = 1.0
            ones[0, base + 3 + part] = 1.0
            ones[0, koff + base + part] = 1.0
            sel[part * heads + h, koff + base + 3 + part] = -1.0
    return jnp.asarray(sel, BF16), jnp.asarray(ones, F32)


def _foxprep_tile(fl, q_ref, k_ref, bf_ref, sel_ref, ones_ref, qa_ref, ka_ref, carry, cum_scr, heads):
    rows = fl.shape[0]
    x = fl + bf_ref[...]
    lane = lax.broadcasted_iota(jnp.int32, (rows, LANES), 1)
    ls = jnp.minimum(x, 0.0) - jnp.log1p(jnp.exp(-jnp.abs(x)))
    ls = jnp.where(lane < heads, ls, 0.0)

    r = lax.broadcasted_iota(jnp.int32, (LANES, LANES), 0)
    c = lax.broadcasted_iota(jnp.int32, (LANES, LANES), 1)
    ltri = jnp.where(c <= r, 1.0, 0.0).astype(BF16)
    run = carry[...]
    for j in range(rows // LANES):
        blk = ls[j * LANES:(j + 1) * LANES]
        parts = jnp.dot(ltri, jnp.concatenate(_split3(blk), axis=1), preferred_element_type=F32)
        loc = parts[:, 0:LANES] + parts[:, LANES:2 * LANES] + parts[:, 2 * LANES:3 * LANES]
        cum_blk = loc + run
        run = cum_blk[LANES - 1:LANES, :]
        cum_scr[j * LANES:(j + 1) * LANES, :] = cum_blk
    carry[...] = run

    chi, cmid, clo = _split3(cum_scr[...] * LOG2E)
    c24 = (chi.astype(F32) + pltpu.roll(cmid.astype(F32), heads, 1)
           + pltpu.roll(clo.astype(F32), 2 * heads, 1)).astype(BF16)
    bias = jnp.dot(c24, sel_ref[...], preferred_element_type=F32) + ones_ref[...]
    bias = bias.astype(BF16)
    koff = heads * LANES
    for h in range(heads):
        pair = (h // 2) * LANES
        keep = (lane < HEAD_DIM) if h % 2 == 0 else (lane >= HEAD_DIM)
        qa_ref[0, h] = jnp.where(keep, q_ref[:, pair:pair + LANES],
                                 bias[:, h * LANES:(h + 1) * LANES])
        ka_ref[0, h] = jnp.where(keep, k_ref[:, pair:pair + LANES],
                                 bias[:, koff + h * LANES:koff + (h + 1) * LANES])


def _premix_kernel(x_ref, mod_ref, w_ref, wf_ref, lng_ref, lnb_ref, ws_ref, bs_ref,
                   bf_ref, sel_ref, ones_ref,
                   ygm_ref, s5a_ref, s5b_ref, v_ref, qa_ref, ka_ref,
                   h_scr, gu_scr, gv_scr, q_scr, k_scr, carry, cum_scr,
                   *, gm, s5w, fw, gm_heads, fox_heads):
    @pl.when(pl.program_id(1) == 0)
    def _():
        carry[...] = jnp.zeros_like(carry)

    sh = mod_ref[0, 0:1, :]
    sc = mod_ref[0, 1:2, :]
    h_scr[...] = (x_ref[0] * (1.0 + sc) + sh).astype(BF16)
    h = h_scr[...]

    def proj(lo, hi):
        return jnp.dot(h, w_ref[:, lo:hi], preferred_element_type=F32)

    o = 0
    gu_scr[...] = proj(o, o + gm); o += gm
    gv_scr[...] = proj(o, o + gm); o += gm
    u = proj(o, o + s5w); o += s5w
    rows = u.shape[0]
    tiles_per_block = S5_NCHUNK * S5_CHUNK // rows
    chunk0 = (pl.program_id(1) % tiles_per_block) * (rows // S5_CHUNK)
    for cl in range(rows // S5_CHUNK):
        for hf, ref in enumerate((s5a_ref, s5b_ref)):
            ref[0, pl.ds(chunk0 + cl, S5_CHUNK, stride=S5_NCHUNK), :] = (
                u[cl * S5_CHUNK:(cl + 1) * S5_CHUNK, hf * LANES:(hf + 1) * LANES])
    q_scr[...] = (proj(o, o + fw) * (LOG2E * HEAD_DIM ** -0.5)).astype(BF16); o += fw
    k_scr[...] = proj(o, o + fw).astype(BF16); o += fw
    v_ref[0] = proj(o, o + fw).astype(BF16)
    fl = jnp.dot(h, wf_ref[...], preferred_element_type=F32)

    _gmlp_tile(gu_scr, gv_scr, lng_ref, lnb_ref, ws_ref, bs_ref, ygm_ref, gm_heads)
    _foxprep_tile(fl, q_scr, k_scr, bf_ref, sel_ref, ones_ref, qa_ref, ka_ref, carry, cum_scr,
                  fox_heads)


def _premix(x, mod, w_in, ln_g_row, ln_b_row, w_s, b_s_full, bf_row, sel, ones, layer,
            gm, s5w, fw):
    b, s, d = x.shape
    tm = ROW_TILE
    n_main = 2 * gm + s5w + 3 * fw
    w_main_spec = _layer_spec(w_in, layer, (d, n_main))
    w_f_spec = pl.BlockSpec((None, d, LANES), lambda *_: (layer, 0, n_main // LANES))
    gm_heads = w_s.shape[1]
    fox_heads = fw // HEAD_DIM
    row = lambda n: pl.BlockSpec((1, tm, n), lambda i, j: (i, j, 0))
    full = lambda a: pl.BlockSpec(a.shape, lambda i, j: (0,) * a.ndim)
    aug = pl.BlockSpec((1, fox_heads, tm, LANES), lambda i, j: (i, 0, j, 0))
    lay = lambda a: _layer_spec(a, layer)
    assert s5w == 2 * LANES
    m_rows = S5_NCHUNK * S5_CHUNK
    s5_half = pl.BlockSpec((1, m_rows, LANES), lambda i, j: (i, j * tm // m_rows, 0))
    outs = [jax.ShapeDtypeStruct((b, s, gm), BF16),
            jax.ShapeDtypeStruct((b, s, LANES), F32), jax.ShapeDtypeStruct((b, s, LANES), F32),
            jax.ShapeDtypeStruct((b, s, fw), BF16),
            jax.ShapeDtypeStruct((b, fox_heads, s, LANES), BF16),
            jax.ShapeDtypeStruct((b, fox_heads, s, LANES), BF16)]
    return pl.pallas_call(
        functools.partial(_premix_kernel, gm=gm, s5w=s5w, fw=fw, gm_heads=gm_heads,
                          fox_heads=fox_heads),
        grid=(b, s // tm),
        in_specs=[row(d), _mod_spec(mod, layer), w_main_spec, w_f_spec, lay(ln_g_row),
                  lay(ln_b_row), lay(w_s), lay(b_s_full), lay(bf_row), full(sel), full(ones)],
        out_specs=[row(gm), s5_half, s5_half, row(fw), aug, aug],
        out_shape=outs,
        scratch_shapes=[pltpu.VMEM((tm, d), BF16),
                        pltpu.VMEM((tm, gm), F32), pltpu.VMEM((tm, gm), F32),
                        pltpu.VMEM((tm, fw), BF16), pltpu.VMEM((tm, fw), BF16),
                        pltpu.VMEM((1, LANES), F32), pltpu.VMEM((tm, LANES), F32)],
        compiler_params=_cparams(2),
        name="premix",
    )(x, mod, w_in, w_in, ln_g_row, ln_b_row, w_s, b_s_full, bf_row, sel, ones)


def _fox_kernel(qa_ref, ka_ref, v_ref, o_ref, va_scr):
    qi = pl.program_id(2)
    nh = qa_ref.shape[1]
    tq = qa_ref.shape[2]
    tk = tq
    nt = (((1,), (1,)), ((), ()))

    @pl.when(qi == 0)
    def _():
        lane_v = lax.broadcasted_iota(jnp.int32, va_scr.shape[1:], 1)
        one = jnp.ones((), BF16)
        for hp in range(nh // 2):
            v2 = v_ref[0, :, hp * LANES:(hp + 1) * LANES]
            va_scr[2 * hp] = jnp.where(lane_v < HEAD_DIM, v2, one)
            va_scr[2 * hp + 1] = jnp.where(lane_v >= HEAD_DIM, v2, one)

    def step(k0, width, state, mask_off):
        logits = [lax.dot_general(qa_ref[0, hh], ka_ref[0, hh, pl.ds(k0, width), :], nt,
                                  preferred_element_type=F32) for hh in range(nh)]
        new = []
        for hh in range(nh):
            m, acc = state[hh]
            s = logits[hh]
            if mask_off is not None:
                row = lax.broadcasted_iota(jnp.int32, (tq, width), 0)
                col = lax.broadcasted_iota(jnp.int32, (tq, width), 1)
                s = jnp.where(col <= row + mask_off, s, NEG_INF)
            m_new = jnp.maximum(m, jnp.max(s, axis=-1, keepdims=True))
            alpha = jnp.exp2(m - m_new)
            p = jnp.exp2(s - m_new)
            pv = jnp.dot(p.astype(BF16), va_scr[hh, pl.ds(k0, width), :],
                         preferred_element_type=F32)
            new.append((m_new, alpha * acc + pv))
        return tuple(new)

    wide = 2 * tk
    n_wide = qi // 2
    init = tuple((jnp.full((tq, 1), -jnp.inf, F32), jnp.zeros((tq, LANES), F32))
                 for _ in range(nh))
    state = lax.fori_loop(
        0, n_wide, lambda j, st: step(pl.multiple_of(j * wide, wide), wide, st, None), init)
    k_tail = pl.multiple_of(n_wide * wide, wide)
    state = lax.cond(qi % 2 == 1,
                     lambda st: step(k_tail, wide, st, tk),
                     lambda st: step(k_tail, tk, st, 0),
                     state)
    outs = [acc / pltpu.roll(acc, HEAD_DIM, 1) for _, acc in state]
    lane = lax.broadcasted_iota(jnp.int32, (tq, LANES), 1)
    for hp in range(nh // 2):
        o_ref[0, :, hp * LANES:(hp + 1) * LANES] = jnp.where(
            lane < HEAD_DIM, outs[2 * hp], outs[2 * hp + 1]).astype(BF16)


def _fox(q_aug, k_aug, v):
    b, heads, s, _ = q_aug.shape
    fw = v.shape[-1]
    tq = FOX_BLOCK
    nh = FOX_HEADS_PER_STEP
    vw = nh // 2 * LANES
    return pl.pallas_call(
        _fox_kernel,
        grid=(b, heads // nh, s // tq),
        in_specs=[pl.BlockSpec((1, nh, tq, LANES), lambda i, p, j: (i, p, j, 0)),
                  pl.BlockSpec((1, nh, s, LANES), lambda i, p, j: (i, p, 0, 0)),
                  pl.BlockSpec((1, s, vw), lambda i, p, j: (i, 0, p))],
        out_specs=pl.BlockSpec((1, tq, vw), lambda i, p, j: (i, j, p)),
        out_shape=jax.ShapeDtypeStruct((b, s, fw), BF16),
        scratch_shapes=[pltpu.VMEM((nh, s, LANES), BF16)],
        compiler_params=_cparams(3),
        name="fox",
    )(q_aug, k_aug, v)


def _mixffn_kernel(ygm_ref, ys5_ref, yfox_ref, wout_ref, x_ref, mod_ref, g1_ref, b1_ref,
                   wup_ref, cw_ref, cb_ref, wdn_ref, g2_ref, b2_ref, o_ref,
                   x1_scr, h_scr, halo, act_scr, *, alpha, gm, s5w, dff, chunk):
    @pl.when(pl.program_id(1) == 0)
    def _():
        halo[...] = jnp.zeros_like(halo)

    mix = jnp.dot(ygm_ref[0], wout_ref[0:gm, :], preferred_element_type=F32)
    mix += jnp.dot(ys5_ref[0], wout_ref[gm:gm + s5w, :], preferred_element_type=F32)
    mix += jnp.dot(yfox_ref[0], wout_ref[gm + s5w:, :], preferred_element_type=F32)
    gate1 = mod_ref[0, 2:3, :]
    x1_scr[...] = _layer_norm_rows(alpha * x_ref[0] + (1.0 + gate1) * mix, g1_ref[...], b1_ref[...])

    rows = x_ref.shape[1]
    sh = mod_ref[0, 3:4, :]
    sc = mod_ref[0, 4:5, :]
    gate = mod_ref[0, 5:6, :]
    h_scr[...] = (x1_scr[...] * (1.0 + sc) + sh).astype(BF16)
    h = h_scr[...]
    rid = lax.broadcasted_iota(jnp.int32, (rows, chunk), 0)
    for ci in range(dff // chunk):
        c0 = ci * chunk
        a = jnp.dot(h, wup_ref[:, c0:c0 + chunk], preferred_element_type=F32)
        gt = jnp.dot(h, wup_ref[:, dff + c0:dff + c0 + chunk], preferred_element_type=F32)
        p1 = halo[SUBLANES - 1:SUBLANES, c0:c0 + chunk]
        p2 = halo[SUBLANES - 2:SUBLANES - 1, c0:c0 + chunk]
        a1 = jnp.where(rid == 0, p1, pltpu.roll(a, 1, 0))
        a2 = jnp.where(rid == 0, p2, jnp.where(rid == 1, p1, pltpu.roll(a, 2, 0)))
        halo[:, c0:c0 + chunk] = a[rows - SUBLANES:rows, :]
        conv = (cb_ref[:, c0:c0 + chunk] + cw_ref[0:1, c0:c0 + chunk] * a2
                + cw_ref[1:2, c0:c0 + chunk] * a1 + cw_ref[2:3, c0:c0 + chunk] * a)
        act_scr[:, c0:c0 + chunk] = (_gelu_tanh(conv) * gt).astype(BF16)
    ffn = jnp.dot(act_scr[...], wdn_ref[...], preferred_element_type=F32)
    y = alpha * x1_scr[...] + (1.0 + gate) * ffn
    o_ref[0] = _layer_norm_rows(y, g2_ref[...], b2_ref[...])


def _mixffn(y_gm, y_s5, y_fox, w_out, x, mod, g1_row, b1_row,
            w_up, conv_w, conv_b_row, w_down, g2_row, b2_row, layer, alpha):
    b, s, d = x.shape
    gm, s5w, fw = y_gm.shape[-1], y_s5.shape[-1], y_fox.shape[-1]
    dff = w_down.shape[1]
    tm = FFN_ROW_TILE
    chunk = 256
    row = lambda n: pl.BlockSpec((1, tm, n), lambda i, j: (i, j, 0))
    full = lambda a: _layer_spec(a, layer)
    resident = lambda a: pl.BlockSpec((None,) + a.shape[1:], lambda i, j: (layer,) + (0,) * (a.ndim - 1),
                                      pipeline_mode=pl.Buffered(1))
    return pl.pallas_call(
        functools.partial(_mixffn_kernel, alpha=alpha, gm=gm, s5w=s5w, dff=dff, chunk=chunk),
        grid=(b, s // tm),
        in_specs=[row(gm), row(s5w), row(fw), resident(w_out), row(d),
                  _mod_spec(mod, layer), full(g1_row), full(b1_row),
                  resident(w_up), full(conv_w), full(conv_b_row), resident(w_down),
                  full(g2_row), full(b2_row)],
        out_specs=row(d),
        out_shape=jax.ShapeDtypeStruct((b, s, d), F32),
        scratch_shapes=[pltpu.VMEM((tm, d), F32),
                        pltpu.VMEM((tm, d), BF16),
                        pltpu.VMEM((SUBLANES, dff), F32),
                        pltpu.VMEM((tm, dff), BF16)],
        compiler_params=_cparams(2),
        name="mixffn",
    )(y_gm, y_s5, y_fox, w_out, x, mod, g1_row, b1_row,
      w_up, conv_w, conv_b_row, w_down, g2_row, b2_row)


def _block_diag(a):
    depth, g, r, c = a.shape
    eye = jnp.eye(g, dtype=a.dtype)
    return (a[:, :, :, None, :] * eye[None, :, None, :, None]).reshape(depth, g * r, g * c)


def kernel(x, c, w_ada, b_ada, w_in, b_f, gm_ln_g, gm_ln_b, gm_w_s, gm_b_s, s5_lam_re, s5_lam_im, s5_log_dt, s5_b_re, s5_b_im, s5_c_re, s5_c_im, s5_d, s5_w_glu, s5_b_glu, w_out, ln1_g, ln1_b, w_up, conv_w, conv_b, w_down, ln2_g, ln2_b):
    depth, d, _ = w_in.shape
    batch = x.shape[0]
    gm_heads = gm_w_s.shape[1]
    gm = gm_heads * HEAD_DIM
    groups = s5_lam_re.shape[1]
    s5w = groups * S5_GROUP_DIM
    fox_heads = b_f.shape[1]
    fw = fox_heads * HEAD_DIM
    n_main = 2 * gm + s5w + 3 * fw
    alpha = (2.0 * depth) ** 0.25

    pad_rows = 2 * SUBLANES
    c_pad = jnp.pad(c, ((0, pad_rows - batch), (0, 0)))
    mod = _adaln(c_pad, w_ada, b_ada[:, None, :])[:, :batch]
    mod = mod.reshape(depth, batch, 6, d)

    n_state = groups * S5_STATE
    lre = s5_lam_re.reshape(depth, 1, n_state)
    lim = s5_lam_im.reshape(depth, 1, n_state)
    ldt = jnp.repeat(s5_log_dt, S5_STATE, axis=-1).reshape(depth, 1, n_state)
    bre_blk = _block_diag(jnp.swapaxes(s5_b_re, -1, -2))
    bim_blk = _block_diag(jnp.swapaxes(s5_b_im, -1, -2))
    cre_blk = _block_diag(jnp.swapaxes(s5_c_re, -1, -2))
    cim_blk = _block_diag(jnp.swapaxes(s5_c_im, -1, -2))
    lam, laml, bblk, cblk = _s5prep(lre, lim, ldt, bre_blk, bim_blk, cre_blk, cim_blk)
    wglu_blk = _block_diag(jnp.swapaxes(s5_w_glu, -1, -2)).astype(BF16)

    w_in_bf = jnp.pad(w_in, ((0, 0), (0, 0), (0, LANES - fox_heads))).astype(BF16)
    w_out_bf = w_out.astype(BF16)
    w_up_bf = w_up.astype(BF16)
    w_down_bf = w_down.astype(BF16)
    row3 = lambda a: a.reshape(depth, 1, -1)
    gm_g, gm_b = row3(gm_ln_g), row3(gm_ln_b)
    bs_full = jnp.repeat(jnp.swapaxes(gm_b_s, 1, 2), HEAD_DIM, axis=2)
    s5_d_row, s5_bglu_row = row3(s5_d), row3(s5_b_glu)
    bf_row = jnp.pad(b_f[:, None, :], ((0, 0), (0, 0), (0, LANES - fox_heads)))
    sel, ones = _fox_bias_tables(fox_heads)
    ln1_g3, ln1_b3, ln2_g3, ln2_b3 = row3(ln1_g), row3(ln1_b), row3(ln2_g), row3(ln2_b)
    conv_b3 = row3(conv_b)

    for l in range(depth):
        y_gm, s5_a, s5_b, v, q_aug, k_aug = _premix(x, mod, w_in_bf, gm_g, gm_b, gm_w_s, bs_full,
                                                    bf_row, sel, ones, l, gm, s5w, fw)
        y_s5 = _s5(s5_a, s5_b, lam, laml, bblk, cblk, l, s5_d_row, wglu_blk, s5_bglu_row)
        y_fox = _fox(q_aug, k_aug, v)
        x = _mixffn(y_gm, y_s5, y_fox, w_out_bf, x, mod, ln1_g3, ln1_b3,
                    w_up_bf, conv_w, conv_b3, w_down_bf, ln2_g3, ln2_b3, l, alpha)
    return x
```

```python
import functools
import math

import numpy as np
import jax
import jax.numpy as jnp
from jax import lax
from jax.experimental import pallas as pl
from jax.experimental.pallas import tpu as pltpu

F32 = jnp.float32
BF16 = jnp.bfloat16

HEAD_DIM = 64
GM_CHUNK = 128
S5_GROUP_DIM = 16
S5_STATE = 64
CONV_WIDTH = 3
LN_EPS = 1e-5
NEG_INF = -1e30
LOG2E = math.log2(math.e)

LANES = 128
SUBLANES = 8
ROW_TILE = 512
FFN_ROW_TILE = 512
S5_CHUNK = 128
S5_NCHUNK = SUBLANES
FOX_BLOCK = 512
FOX_HEADS_PER_STEP = 2
VMEM_LIMIT = 56 * 1024 * 1024


def _cparams(n_axes, vmem=VMEM_LIMIT, flags=None):
    return pltpu.CompilerParams(dimension_semantics=("arbitrary",) * n_axes,
                                vmem_limit_bytes=vmem, flags=flags)


def _layer_norm_rows(y, g, b):
    mu = jnp.mean(y, axis=-1, keepdims=True)
    d = y - mu
    var = jnp.mean(d * d, axis=-1, keepdims=True)
    return d * lax.rsqrt(var + LN_EPS) * g + b


def _gelu_tanh(x):
    c = math.sqrt(2.0 / math.pi)
    return 0.5 * x * (1.0 + jnp.tanh(c * (x + 0.044715 * (x * x * x))))


def _split3(x):
    hi = x.astype(BF16)
    r1 = x - hi.astype(F32)
    mid = r1.astype(BF16)
    lo = (r1 - mid.astype(F32)).astype(BF16)
    return hi, mid, lo


def _adaln_kernel(c_ref, w_ref, b_ref, o_ref):
    c = c_ref[...]
    cond = (c * jax.nn.sigmoid(c)).astype(BF16)
    w = w_ref[0].astype(BF16)
    o_ref[0] = jnp.dot(cond, w, preferred_element_type=F32) + b_ref[0]


def _adaln(c_pad, w_ada, b_ada3):
    depth, d, n = w_ada.shape
    rows = c_pad.shape[0]
    tn = 1536
    return pl.pallas_call(
        _adaln_kernel,
        grid=(depth, n // tn),
        in_specs=[pl.BlockSpec((rows, d), lambda l, j: (0, 0)),
                  pl.BlockSpec((1, d, tn), lambda l, j: (l, 0, j)),
                  pl.BlockSpec((1, 1, tn), lambda l, j: (l, 0, j))],
        out_specs=pl.BlockSpec((1, rows, tn), lambda l, j: (l, 0, j)),
        out_shape=jax.ShapeDtypeStruct((depth, rows, n), F32),
        compiler_params=_cparams(2),
        name="adaln",
    )(c_pad, w_ada, b_ada3)


def _layer_spec(a, layer, block=None):
    shape = tuple(a.shape[1:]) if block is None else tuple(block)
    return pl.BlockSpec((None,) + shape, lambda *_: (layer,) + (0,) * len(shape))


def _mod_spec(mod, layer):
    return pl.BlockSpec((None, 1) + tuple(mod.shape[2:]), lambda i, *_: (layer, i, 0, 0))


def _gmlp_tile(u_ref, v_ref, g_ref, b_ref, ws_ref, bs_ref, o_ref, heads):
    v = v_ref[...]
    rows, width = v.shape
    lane = lax.broadcasted_iota(jnp.int32, (rows, width), 1)
    head_masks = [(lane >= h * HEAD_DIM) & (lane < (h + 1) * HEAD_DIM) for h in range(heads)]

    def seg_mean(a):
        out = jnp.zeros_like(a)
        for m in head_masks:
            s = jnp.sum(jnp.where(m, a, 0.0), axis=-1, keepdims=True) * (1.0 / HEAD_DIM)
            out = jnp.where(m, s, out)
        return out

    d = v - seg_mean(v)
    var = seg_mean(d * d)
    vn = (d * lax.rsqrt(var + LN_EPS) * g_ref[...] + b_ref[...]).astype(BF16)

    r = lax.broadcasted_iota(jnp.int32, (GM_CHUNK, GM_CHUNK), 0)
    c = lax.broadcasted_iota(jnp.int32, (GM_CHUNK, GM_CHUNK), 1)
    tril = c <= r
    ws = [jnp.where(tril, ws_ref[h], 0.0).astype(BF16) for h in range(heads)]
    lane_c = lax.broadcasted_iota(jnp.int32, (GM_CHUNK, width), 1)
    chunk_masks = [(lane_c >= h * HEAD_DIM) & (lane_c < (h + 1) * HEAD_DIM) for h in range(heads)]
    bs = bs_ref[...]
    for j in range(rows // GM_CHUNK):
        r0 = j * GM_CHUNK
        vc = vn[r0:r0 + GM_CHUNK]
        z = jnp.zeros((GM_CHUNK, width), F32)
        for h in range(heads):
            zh = jnp.dot(ws[h], vc, preferred_element_type=F32)
            z = jnp.where(chunk_masks[h], zh, z)
        o_ref[0, r0:r0 + GM_CHUNK, :] = (u_ref[r0:r0 + GM_CHUNK, :] * (z + bs)).astype(BF16)


def _s5prep_kernel(lre_ref, lim_ref, ldt_ref, bre_ref, bim_ref, cre_ref, cim_ref,
                   lam_ref, laml_ref, bblk_ref, cblk_ref):
    lre = lre_ref[0]
    lim = lim_ref[0]
    dt = jnp.exp(ldt_ref[0])
    mag = jnp.exp(lre * dt)
    lbr = mag * jnp.cos(lim * dt)
    lbi = mag * jnp.sin(lim * dt)
    nr = lbr - 1.0
    den = lre * lre + lim * lim
    cr = (nr * lre + lbi * lim) / den
    ci = (lbi * lre - nr * lim) / den
    n = lre.shape[-1]
    lam_ref[0, :, 0:n] = lbr
    lam_ref[0, :, n:2 * n] = lbi
    pr, pi = lbr, lbi
    for _ in range(int(math.log2(S5_CHUNK))):
        pr, pi = pr * pr - pi * pi, 2.0 * (pr * pi)
    laml_ref[0, :, 0:n] = pr
    laml_ref[0, :, n:2 * n] = pi
    bre = bre_ref[0]
    bim = bim_ref[0]
    bblk_ref[0, :, 0:n] = (cr * bre - ci * bim).astype(BF16)
    bblk_ref[0, :, n:2 * n] = (cr * bim + ci * bre).astype(BF16)
    cblk_ref[0, 0:n, :] = cre_ref[0].astype(BF16)
    cblk_ref[0, n:2 * n, :] = (-cim_ref[0]).astype(BF16)


def _s5prep(lre, lim, ldt, bre_blk, bim_blk, cre_blk, cim_blk):
    depth, _, n = lre.shape
    w = bre_blk.shape[1]
    vec = pl.BlockSpec((1, 1, n), lambda l: (l, 0, 0))
    return pl.pallas_call(
        _s5prep_kernel,
        grid=(depth,),
        in_specs=[vec, vec, vec,
                  pl.BlockSpec((1, w, n), lambda l: (l, 0, 0)),
                  pl.BlockSpec((1, w, n), lambda l: (l, 0, 0)),
                  pl.BlockSpec((1, n, w), lambda l: (l, 0, 0)),
                  pl.BlockSpec((1, n, w), lambda l: (l, 0, 0))],
        out_specs=[pl.BlockSpec((1, 1, 2 * n), lambda l: (l, 0, 0)),
                   pl.BlockSpec((1, 1, 2 * n), lambda l: (l, 0, 0)),
                   pl.BlockSpec((1, w, 2 * n), lambda l: (l, 0, 0)),
                   pl.BlockSpec((1, 2 * n, w), lambda l: (l, 0, 0))],
        out_shape=[jax.ShapeDtypeStruct((depth, 1, 2 * n), F32),
                   jax.ShapeDtypeStruct((depth, 1, 2 * n), F32),
                   jax.ShapeDtypeStruct((depth, w, 2 * n), BF16),
                   jax.ShapeDtypeStruct((depth, 2 * n, w), BF16)],
        compiler_params=_cparams(1),
        name="s5prep",
    )(lre, lim, ldt, bre_blk, bim_blk, cre_blk, cim_blk)


def _s5_kernel(ua_ref, ub_ref, lam_ref, laml_ref, bblk_ref, cblk_ref, d_ref, wglu_ref, bglu_ref,
               o_ref, r_scr, xs_scr, e_scr, xin_scr, carry, y_scr, *, n):
    nc, lc = S5_NCHUNK, S5_CHUNK
    m_rows = nc * lc

    @pl.when(pl.program_id(1) == 0)
    def _():
        carry[...] = jnp.zeros_like(carry)

    u_rows = lambda r0, r1: jnp.concatenate([ua_ref[0, r0:r1, :], ub_ref[0, r0:r1, :]], axis=1)
    r_scr[...] = jnp.dot(u_rows(0, m_rows).astype(BF16), bblk_ref[0],
                         preferred_element_type=F32)

    lr = jnp.broadcast_to(lam_ref[0, :, 0:n], (nc, n))
    li = jnp.broadcast_to(lam_ref[0, :, n:2 * n], (nc, n))

    def scan(x0r, x0i, store):
        def body(t2, c):
            xr, xi = c
            r0 = pl.multiple_of(t2 * (2 * nc), 2 * nc)
            kept = []
            for half in range(2):
                br = r_scr[pl.ds(r0 + half * nc, nc), 0:n]
                bi = r_scr[pl.ds(r0 + half * nc, nc), n:2 * n]
                xr, xi = lr * xr - li * xi + br, lr * xi + li * xr + bi
                kept.append((xr, xi))
            if store:
                xs_scr[pl.ds(r0, 2 * nc), 0:n] = jnp.concatenate(
                    [kept[0][0], kept[1][0]], axis=0).astype(BF16)
                xs_scr[pl.ds(r0, 2 * nc), n:2 * n] = jnp.concatenate(
                    [kept[0][1], kept[1][1]], axis=0).astype(BF16)
            return xr, xi
        return lax.fori_loop(0, lc // 2, body, (x0r, x0i))

    zero = jnp.zeros((nc, n), F32)
    er, ei = scan(zero, zero, False)
    e_scr[:, 0:n] = er
    e_scr[:, n:2 * n] = ei

    llr = laml_ref[0, :, 0:n]
    lli = laml_ref[0, :, n:2 * n]
    cur_r = carry[:, 0:n]
    cur_i = carry[:, n:2 * n]
    for c in range(nc):
        xin_scr[c:c + 1, 0:n] = cur_r
        xin_scr[c:c + 1, n:2 * n] = cur_i
        e_r = e_scr[c:c + 1, 0:n]
        e_i = e_scr[c:c + 1, n:2 * n]
        cur_r, cur_i = llr * cur_r - lli * cur_i + e_r, llr * cur_i + lli * cur_r + e_i
    carry[:, 0:n] = cur_r
    carry[:, n:2 * n] = cur_i

    scan(xin_scr[:, 0:n], xin_scr[:, n:2 * n], True)

    step = 256
    for r0 in range(0, m_rows, step):
        y = jnp.dot(xs_scr[r0:r0 + step, :], cblk_ref[0], preferred_element_type=F32)
        y = y + d_ref[...] * u_rows(r0, r0 + step)
        y = _gelu_tanh(y)
        gate = jnp.dot(y.astype(BF16), wglu_ref[...], preferred_element_type=F32) + bglu_ref[...]
        out = y * jax.nn.sigmoid(gate)
        for hf in range(out.shape[1] // LANES):
            y_scr[hf, r0:r0 + step, :] = out[:, hf * LANES:(hf + 1) * LANES]

    for c in range(nc):
        for hf in range(y_scr.shape[0]):
            o_ref[0, c * lc:(c + 1) * lc, hf * LANES:(hf + 1) * LANES] = (
                y_scr[hf, pl.ds(c, lc, stride=nc), :].astype(BF16))


def _s5(u_a, u_b, lam, laml, bblk, cblk, layer, d_row, wglu_blk, bglu_row):
    b, s, _ = u_a.shape
    w = 2 * LANES
    n = lam.shape[-1] // 2
    m_rows = S5_NCHUNK * S5_CHUNK
    per_layer = lambda a: pl.BlockSpec((1,) + a.shape[1:], lambda i, j: (layer,) + (0,) * (a.ndim - 1))
    half = pl.BlockSpec((1, m_rows, LANES), lambda i, j: (i, j, 0))
    return pl.pallas_call(
        functools.partial(_s5_kernel, n=n),
        grid=(b, s // m_rows),
        in_specs=[half, half,
                  per_layer(lam), per_layer(laml), per_layer(bblk), per_layer(cblk),
                  _layer_spec(d_row, layer), _layer_spec(wglu_blk, layer),
                  _layer_spec(bglu_row, layer)],
        out_specs=pl.BlockSpec((1, m_rows, w), lambda i, j: (i, j, 0)),
        out_shape=jax.ShapeDtypeStruct((b, s, w), BF16),
        scratch_shapes=[pltpu.VMEM((m_rows, 2 * n), F32),
                        pltpu.VMEM((m_rows, 2 * n), BF16),
                        pltpu.VMEM((S5_NCHUNK, 2 * n), F32),
                        pltpu.VMEM((S5_NCHUNK, 2 * n), F32),
                        pltpu.VMEM((1, 2 * n), F32),
                        pltpu.VMEM((w // LANES, m_rows, LANES), F32)],
        compiler_params=_cparams(2),
        name="s5",
    )(u_a, u_b, lam, laml, bblk, cblk, d_row, wglu_blk, bglu_row)


def _fox_bias_tables(heads):
    sel = np.zeros((LANES, 2 * heads * LANES), np.float32)
    ones = np.zeros((1, 2 * heads * LANES), np.float32)
    koff = heads * LANES
    for h in range(heads):
        base = h * LANES + (HEAD_DIM if h % 2 == 0 else 0)
        for part in range(3):
            sel[part * heads + h, base + part] = 1.0
            ones[0, base + 3 + part] = 1.0
            ones[0, koff + base + part] = 1.0
            sel[part * heads + h, koff + base + 3 + part] = -1.0
    return jnp.asarray(sel, BF16), jnp.asarray(ones, F32)


def _foxprep_tile(fl, q_ref, k_ref, bf_ref, sel_ref, ones_ref, qa_ref, ka_ref, carry, cum_scr, heads):
    rows = fl.shape[0]
    x = fl + bf_ref[...]
    lane = lax.broadcasted_iota(jnp.int32, (rows, LANES), 1)
    ls = jnp.minimum(x, 0.0) - jnp.log1p(jnp.exp(-jnp.abs(x)))
    ls = jnp.where(lane < heads, ls, 0.0)

    r = lax.broadcasted_iota(jnp.int32, (LANES, LANES), 0)
    c = lax.broadcasted_iota(jnp.int32, (LANES, LANES), 1)
    ltri = jnp.where(c <= r, 1.0, 0.0).astype(BF16)
    run = carry[...]
    for j in range(rows // LANES):
        blk = ls[j * LANES:(j + 1) * LANES]
        parts = jnp.dot(ltri, jnp.concatenate(_split3(blk), axis=1), preferred_element_type=F32)
        loc = parts[:, 0:LANES] + parts[:, LANES:2 * LANES] + parts[:, 2 * LANES:3 * LANES]
        cum_blk = loc + run
        run = cum_blk[LANES - 1:LANES, :]
        cum_scr[j * LANES:(j + 1) * LANES, :] = cum_blk
    carry[...] = run

    chi, cmid, clo = _split3(cum_scr[...] * LOG2E)
    c24 = (chi.astype(F32) + pltpu.roll(cmid.astype(F32), heads, 1)
           + pltpu.roll(clo.astype(F32), 2 * heads, 1)).astype(BF16)
    bias = jnp.dot(c24, sel_ref[...], preferred_element_type=F32) + ones_ref[...]
    bias = bias.astype(BF16)
    koff = heads * LANES
    for h in range(heads):
        pair = (h // 2) * LANES
        keep = (lane < HEAD_DIM) if h % 2 == 0 else (lane >= HEAD_DIM)
        qa_ref[0, h] = jnp.where(keep, q_ref[:, pair:pair + LANES],
                                 bias[:, h * LANES:(h + 1) * LANES])
        ka_ref[0, h] = jnp.where(keep, k_ref[:, pair:pair + LANES],
                                 bias[:, koff + h * LANES:koff + (h + 1) * LANES])


def _premix_kernel(x_ref, mod_ref, w_ref, wf_ref, lng_ref, lnb_ref, ws_ref, bs_ref,
                   bf_ref, sel_ref, ones_ref,
                   ygm_ref, s5a_ref, s5b_ref, v_ref, qa_ref, ka_ref,
                   h_scr, gu_scr, gv_scr, q_scr, k_scr, carry, cum_scr,
                   *, gm, s5w, fw, gm_heads, fox_heads):
    @pl.when(pl.program_id(1) == 0)
    def _():
        carry[...] = jnp.zeros_like(carry)

    sh = mod_ref[0, 0:1, :]
    sc = mod_ref[0, 1:2, :]
    h_scr[...] = (x_ref[0] * (1.0 + sc) + sh).astype(BF16)
    h = h_scr[...]

    def proj(lo, hi):
        return jnp.dot(h, w_ref[:, lo:hi], preferred_element_type=F32)

    o = 0
    gu_scr[...] = proj(o, o + gm); o += gm
    gv_scr[...] = proj(o, o + gm); o += gm
    u = proj(o, o + s5w); o += s5w
    rows = u.shape[0]
    tiles_per_block = S5_NCHUNK * S5_CHUNK // rows
    chunk0 = (pl.program_id(1) % tiles_per_block) * (rows // S5_CHUNK)
    for cl in range(rows // S5_CHUNK):
        for hf, ref in enumerate((s5a_ref, s5b_ref)):
            ref[0, pl.ds(chunk0 + cl, S5_CHUNK, stride=S5_NCHUNK), :] = (
                u[cl * S5_CHUNK:(cl + 1) * S5_CHUNK, hf * LANES:(hf + 1) * LANES])
    q_scr[...] = (proj(o, o + fw) * (LOG2E * HEAD_DIM ** -0.5)).astype(BF16); o += fw
    k_scr[...] = proj(o, o + fw).astype(BF16); o += fw
    v_ref[0] = proj(o, o + fw).astype(BF16)
    fl = jnp.dot(h, wf_ref[...], preferred_element_type=F32)

    _gmlp_tile(gu_scr, gv_scr, lng_ref, lnb_ref, ws_ref, bs_ref, ygm_ref, gm_heads)
    _foxprep_tile(fl, q_scr, k_scr, bf_ref, sel_ref, ones_ref, qa_ref, ka_ref, carry, cum_scr,
                  fox_heads)


def _premix(x, mod, w_in, ln_g_row, ln_b_row, w_s, b_s_full, bf_row, sel, ones, layer,
            gm, s5w, fw):
    b, s, d = x.shape
    tm = ROW_TILE
    n_main = 2 * gm + s5w + 3 * fw
    w_main_spec = _layer_spec(w_in, layer, (d, n_main))
    w_f_spec = pl.BlockSpec((None, d, LANES), lambda *_: (layer, 0, n_main // LANES))
    gm_heads = w_s.shape[1]
    fox_heads = fw // HEAD_DIM
    row = lambda n: pl.BlockSpec((1, tm, n), lambda i, j: (i, j, 0))
    full = lambda a: pl.BlockSpec(a.shape, lambda i, j: (0,) * a.ndim)
    aug = pl.BlockSpec((1, fox_heads, tm, LANES), lambda i, j: (i, 0, j, 0))
    lay = lambda a: _layer_spec(a, layer)
    assert s5w == 2 * LANES
    m_rows = S5_NCHUNK * S5_CHUNK
    s5_half = pl.BlockSpec((1, m_rows, LANES), lambda i, j: (i, j * tm // m_rows, 0))
    outs = [jax.ShapeDtypeStruct((b, s, gm), BF16),
            jax.ShapeDtypeStruct((b, s, LANES), F32), jax.ShapeDtypeStruct((b, s, LANES), F32),
            jax.ShapeDtypeStruct((b, s, fw), BF16),
            jax.ShapeDtypeStruct((b, fox_heads, s, LANES), BF16),
            jax.ShapeDtypeStruct((b, fox_heads, s, LANES), BF16)]
    return pl.pallas_call(
        functools.partial(_premix_kernel, gm=gm, s5w=s5w, fw=fw, gm_heads=gm_heads,
                          fox_heads=fox_heads),
        grid=(b, s // tm),
        in_specs=[row(d), _mod_spec(mod, layer), w_main_spec, w_f_spec, lay(ln_g_row),
                  lay(ln_b_row), lay(w_s), lay(b_s_full), lay(bf_row), full(sel), full(ones)],
        out_specs=[row(gm), s5_half, s5_half, row(fw), aug, aug],
        out_shape=outs,
        scratch_shapes=[pltpu.VMEM((tm, d), BF16),
                        pltpu.VMEM((tm, gm), F32), pltpu.VMEM((tm, gm), F32),
                        pltpu.VMEM((tm, fw), BF16), pltpu.VMEM((tm, fw), BF16),
                        pltpu.VMEM((1, LANES), F32), pltpu.VMEM((tm, LANES), F32)],
        compiler_params=_cparams(2),
        name="premix",
    )(x, mod, w_in, w_in, ln_g_row, ln_b_row, w_s, b_s_full, bf_row, sel, ones)


def _fox_kernel(qa_ref, ka_ref, v_ref, o_ref, va_scr):
    nh = qa_ref.shape[1]
    s_len = qa_ref.shape[2]
    tq = FOX_BLOCK
    tk = tq
    nt = (((1,), (1,)), ((), ()))

    lane_v = lax.broadcasted_iota(jnp.int32, va_scr.shape[1:], 1)
    one = jnp.ones((), BF16)
    for hp in range(nh // 2):
        v2 = v_ref[0, :, hp * LANES:(hp + 1) * LANES]
        va_scr[2 * hp] = jnp.where(lane_v < HEAD_DIM, v2, one)
        va_scr[2 * hp + 1] = jnp.where(lane_v >= HEAD_DIM, v2, one)

    def step(q0, k0, width, state, mask_off):
        logits = [lax.dot_general(qa_ref[0, hh, q0:q0 + tq, :], ka_ref[0, hh, k0:k0 + width, :],
                                  nt, preferred_element_type=F32) for hh in range(nh)]
        new = []
        for hh in range(nh):
            m, acc = state[hh]
            s = logits[hh]
            if mask_off is not None:
                row = lax.broadcasted_iota(jnp.int32, (tq, width), 0)
                col = lax.broadcasted_iota(jnp.int32, (tq, width), 1)
                s = jnp.where(col <= row + mask_off, s, NEG_INF)
            m_new = jnp.maximum(m, jnp.max(s, axis=-1, keepdims=True))
            alpha = jnp.exp2(m - m_new)
            p = jnp.exp2(s - m_new)
            pv = jnp.dot(p.astype(BF16), va_scr[hh, k0:k0 + width, :],
                         preferred_element_type=F32)
            new.append((m_new, alpha * acc + pv))
        return tuple(new)

    wide = 2 * tk
    lane = lax.broadcasted_iota(jnp.int32, (tq, LANES), 1)
    for qi in range(s_len // tq):
        q0 = qi * tq
        state = tuple((jnp.full((tq, 1), -jnp.inf, F32), jnp.zeros((tq, LANES), F32))
                      for _ in range(nh))
        for j in range(qi // 2):
            state = step(q0, j * wide, wide, state, None)
        k_tail = (qi // 2) * wide
        if qi % 2 == 1:
            state = step(q0, k_tail, wide, state, tk)
        else:
            state = step(q0, k_tail, tk, state, 0)
        outs = [acc / pltpu.roll(acc, HEAD_DIM, 1) for _, acc in state]
        for hp in range(nh // 2):
            o_ref[0, q0:q0 + tq, hp * LANES:(hp + 1) * LANES] = jnp.where(
                lane < HEAD_DIM, outs[2 * hp], outs[2 * hp + 1]).astype(BF16)


def _fox(q_aug, k_aug, v):
    b, heads, s, _ = q_aug.shape
    fw = v.shape[-1]
    nh = FOX_HEADS_PER_STEP
    vw = nh // 2 * LANES
    seq = lambda: pl.BlockSpec((1, nh, s, LANES), lambda i, p: (i, p, 0, 0))
    return pl.pallas_call(
        _fox_kernel,
        grid=(b, heads // nh),
        in_specs=[seq(), seq(), pl.BlockSpec((1, s, vw), lambda i, p: (i, 0, p))],
        out_specs=pl.BlockSpec((1, s, vw), lambda i, p: (i, 0, p)),
        out_shape=jax.ShapeDtypeStruct((b, s, fw), BF16),
        scratch_shapes=[pltpu.VMEM((nh, s, LANES), BF16)],
        compiler_params=_cparams(2),
        name="fox",
    )(q_aug, k_aug, v)


def _mixffn_kernel(ygm_ref, ys5_ref, yfox_ref, wout_ref, x_ref, mod_ref, g1_ref, b1_ref,
                   wup_ref, cw_ref, cb_ref, wdn_ref, g2_ref, b2_ref, o_ref,
                   x1_scr, h_scr, halo, act_scr, *, alpha, gm, s5w, dff, chunk):
    @pl.when(pl.program_id(1) == 0)
    def _():
        halo[...] = jnp.zeros_like(halo)

    mix = jnp.dot(ygm_ref[0], wout_ref[0:gm, :], preferred_element_type=F32)
    mix += jnp.dot(ys5_ref[0], wout_ref[gm:gm + s5w, :], preferred_element_type=F32)
    mix += jnp.dot(yfox_ref[0], wout_ref[gm + s5w:, :], preferred_element_type=F32)
    gate1 = mod_ref[0, 2:3, :]
    x1_scr[...] = _layer_norm_rows(alpha * x_ref[0] + (1.0 + gate1) * mix, g1_ref[...], b1_ref[...])

    rows = x_ref.shape[1]
    sh = mod_ref[0, 3:4, :]
    sc = mod_ref[0, 4:5, :]
    gate = mod_ref[0, 5:6, :]
    h_scr[...] = (x1_scr[...] * (1.0 + sc) + sh).astype(BF16)
    h = h_scr[...]
    rid = lax.broadcasted_iota(jnp.int32, (rows, chunk), 0)
    for ci in range(dff // chunk):
        c0 = ci * chunk
        a = jnp.dot(h, wup_ref[:, c0:c0 + chunk], preferred_element_type=F32)
        gt = jnp.dot(h, wup_ref[:, dff + c0:dff + c0 + chunk], preferred_element_type=F32)
        p1 = halo[SUBLANES - 1:SUBLANES, c0:c0 + chunk]
        p2 = halo[SUBLANES - 2:SUBLANES - 1, c0:c0 + chunk]
        a1 = jnp.where(rid == 0, p1, pltpu.roll(a, 1, 0))
        a2 = jnp.where(rid == 0, p2, jnp.where(rid == 1, p1, pltpu.roll(a, 2, 0)))
        halo[:, c0:c0 + chunk] = a[rows - SUBLANES:rows, :]
        conv = (cb_ref[:, c0:c0 + chunk] + cw_ref[0:1, c0:c0 + chunk] * a2
                + cw_ref[1:2, c0:c0 + chunk] * a1 + cw_ref[2:3, c0:c0 + chunk] * a)
        act_scr[:, c0:c0 + chunk] = (_gelu_tanh(conv) * gt).astype(BF16)
    ffn = jnp.dot(act_scr[...], wdn_ref[...], preferred_element_type=F32)
    y = alpha * x1_scr[...] + (1.0 + gate) * ffn
    o_ref[0] = _layer_norm_rows(y, g2_ref[...], b2_ref[...])


def _mixffn(y_gm, y_s5, y_fox, w_out, x, mod, g1_row, b1_row,
            w_up, conv_w, conv_b_row, w_down, g2_row, b2_row, layer, alpha):
    b, s, d = x.shape
    gm, s5w, fw = y_gm.shape[-1], y_s5.shape[-1], y_fox.shape[-1]
    dff = w_down.shape[1]
    tm = FFN_ROW_TILE
    chunk = 256
    row = lambda n: pl.BlockSpec((1, tm, n), lambda i, j: (i, j, 0))
    full = lambda a: _layer_spec(a, layer)
    resident = lambda a: pl.BlockSpec((None,) + a.shape[1:], lambda i, j: (layer,) + (0,) * (a.ndim - 1),
                                      pipeline_mode=pl.Buffered(1))
    return pl.pallas_call(
        functools.partial(_mixffn_kernel, alpha=alpha, gm=gm, s5w=s5w, dff=dff, chunk=chunk),
        grid=(b, s // tm),
        in_specs=[row(gm), row(s5w), row(fw), resident(w_out), row(d),
                  _mod_spec(mod, layer), full(g1_row), full(b1_row),
                  resident(w_up), full(conv_w), full(conv_b_row), resident(w_down),
                  full(g2_row), full(b2_row)],
        out_specs=row(d),
        out_shape=jax.ShapeDtypeStruct((b, s, d), F32),
        scratch_shapes=[pltpu.VMEM((tm, d), F32),
                        pltpu.VMEM((tm, d), BF16),
                        pltpu.VMEM((SUBLANES, dff), F32),
                        pltpu.VMEM((tm, dff), BF16)],
        compiler_params=_cparams(2),
        name="mixffn",
    )(y_gm, y_s5, y_fox, w_out, x, mod, g1_row, b1_row,
      w_up, conv_w, conv_b_row, w_down, g2_row, b2_row)


def _block_diag(a):
    depth, g, r, c = a.shape
    eye = jnp.eye(g, dtype=a.dtype)
    return (a[:, :, :, None, :] * eye[None, :, None, :, None]).reshape(depth, g * r, g * c)


def kernel(x, c, w_ada, b_ada, w_in, b_f, gm_ln_g, gm_ln_b, gm_w_s, gm_b_s, s5_lam_re, s5_lam_im, s5_log_dt, s5_b_re, s5_b_im, s5_c_re, s5_c_im, s5_d, s5_w_glu, s5_b_glu, w_out, ln1_g, ln1_b, w_up, conv_w, conv_b, w_down, ln2_g, ln2_b):
    depth, d, _ = w_in.shape
    batch = x.shape[0]
    gm_heads = gm_w_s.shape[1]
    gm = gm_heads * HEAD_DIM
    groups = s5_lam_re.shape[1]
    s5w = groups * S5_GROUP_DIM
    fox_heads = b_f.shape[1]
    fw = fox_heads * HEAD_DIM
    n_main = 2 * gm + s5w + 3 * fw
    alpha = (2.0 * depth) ** 0.25

    pad_rows = 2 * SUBLANES
    c_pad = jnp.pad(c, ((0, pad_rows - batch), (0, 0)))
    mod = _adaln(c_pad, w_ada, b_ada[:, None, :])[:, :batch]
    mod = mod.reshape(depth, batch, 6, d)

    n_state = groups * S5_STATE
    lre = s5_lam_re.reshape(depth, 1, n_state)
    lim = s5_lam_im.reshape(depth, 1, n_state)
    ldt = jnp.repeat(s5_log_dt, S5_STATE, axis=-1).reshape(depth, 1, n_state)
    bre_blk = _block_diag(jnp.swapaxes(s5_b_re, -1, -2))
    bim_blk = _block_diag(jnp.swapaxes(s5_b_im, -1, -2))
    cre_blk = _block_diag(jnp.swapaxes(s5_c_re, -1, -2))
    cim_blk = _block_diag(jnp.swapaxes(s5_c_im, -1, -2))
    lam, laml, bblk, cblk = _s5prep(lre, lim, ldt, bre_blk, bim_blk, cre_blk, cim_blk)
    wglu_blk = _block_diag(jnp.swapaxes(s5_w_glu, -1, -2)).astype(BF16)

    w_in_bf = jnp.pad(w_in, ((0, 0), (0, 0), (0, LANES - fox_heads))).astype(BF16)
    w_out_bf = w_out.astype(BF16)
    w_up_bf = w_up.astype(BF16)
    w_down_bf = w_down.astype(BF16)
    row3 = lambda a: a.reshape(depth, 1, -1)
    gm_g, gm_b = row3(gm_ln_g), row3(gm_ln_b)
    bs_full = jnp.repeat(jnp.swapaxes(gm_b_s, 1, 2), HEAD_DIM, axis=2)
    s5_d_row, s5_bglu_row = row3(s5_d), row3(s5_b_glu)
    bf_row = jnp.pad(b_f[:, None, :], ((0, 0), (0, 0), (0, LANES - fox_heads)))
    sel, ones = _fox_bias_tables(fox_heads)
    ln1_g3, ln1_b3, ln2_g3, ln2_b3 = row3(ln1_g), row3(ln1_b), row3(ln2_g), row3(ln2_b)
    conv_b3 = row3(conv_b)

    for l in range(depth):
        y_gm, s5_a, s5_b, v, q_aug, k_aug = _premix(x, mod, w_in_bf, gm_g, gm_b, gm_w_s, bs_full,
                                                    bf_row, sel, ones, l, gm, s5w, fw)
        y_s5 = _s5(s5_a, s5_b, lam, laml, bblk, cblk, l, s5_d_row, wglu_blk, s5_bglu_row)
        y_fox = _fox(q_aug, k_aug, v)
        x = _mixffn(y_gm, y_s5, y_fox, w_out_bf, x, mod, ln1_g3, ln1_b3,
                    w_up_bf, conv_w, conv_b3, w_down_bf, ln2_g3, ln2_b3, l, alpha)
    return x
```

```python
import functools
import math

import numpy as np
import jax
import jax.numpy as jnp
from jax import lax
from jax.experimental import pallas as pl
from jax.experimental.pallas import tpu as pltpu

F32 = jnp.float32
BF16 = jnp.bfloat16

HEAD_DIM = 64
GM_CHUNK = 128
S5_GROUP_DIM = 16
S5_STATE = 64
CONV_WIDTH = 3
LN_EPS = 1e-5
NEG_INF = -1e30
LOG2E = math.log2(math.e)

LANES = 128
SUBLANES = 8
ROW_TILE = 512
FFN_ROW_TILE = 512
S5_CHUNK = 128
S5_NCHUNK = SUBLANES
FOX_BLOCK = 512
FOX_HEADS_PER_STEP = 2
VMEM_LIMIT = 56 * 1024 * 1024


def _cparams(n_axes, vmem=VMEM_LIMIT, flags=None):
    return pltpu.CompilerParams(dimension_semantics=("arbitrary",) * n_axes,
                                vmem_limit_bytes=vmem, flags=flags)


def _layer_norm_rows(y, g, b):
    mu = jnp.mean(y, axis=-1, keepdims=True)
    d = y - mu
    var = jnp.mean(d * d, axis=-1, keepdims=True)
    return d * lax.rsqrt(var + LN_EPS) * g + b


def _gelu_tanh(x):
    c = math.sqrt(2.0 / math.pi)
    return 0.5 * x * (1.0 + jnp.tanh(c * (x + 0.044715 * (x * x * x))))


def _split3(x):
    hi = x.astype(BF16)
    r1 = x - hi.astype(F32)
    mid = r1.astype(BF16)
    lo = (r1 - mid.astype(F32)).astype(BF16)
    return hi, mid, lo


def _adaln_kernel(c_ref, w_ref, b_ref, o_ref):
    c = c_ref[...]
    cond = (c * jax.nn.sigmoid(c)).astype(BF16)
    w = w_ref[0].astype(BF16)
    o_ref[0] = jnp.dot(cond, w, preferred_element_type=F32) + b_ref[0]


def _adaln(c_pad, w_ada, b_ada3):
    depth, d, n = w_ada.shape
    rows = c_pad.shape[0]
    tn = 1536
    return pl.pallas_call(
        _adaln_kernel,
        grid=(depth, n // tn),
        in_specs=[pl.BlockSpec((rows, d), lambda l, j: (0, 0)),
                  pl.BlockSpec((1, d, tn), lambda l, j: (l, 0, j)),
                  pl.BlockSpec((1, 1, tn), lambda l, j: (l, 0, j))],
        out_specs=pl.BlockSpec((1, rows, tn), lambda l, j: (l, 0, j)),
        out_shape=jax.ShapeDtypeStruct((depth, rows, n), F32),
        compiler_params=_cparams(2),
        name="adaln",
    )(c_pad, w_ada, b_ada3)


def _layer_spec(a, layer, block=None):
    shape = tuple(a.shape[1:]) if block is None else tuple(block)
    return pl.BlockSpec((None,) + shape, lambda *_: (layer,) + (0,) * len(shape))


def _mod_spec(mod, layer):
    return pl.BlockSpec((None, 1) + tuple(mod.shape[2:]), lambda i, *_: (layer, i, 0, 0))


def _gmlp_tile(u_ref, v_ref, g_ref, b_ref, ws_ref, bs_ref, o_ref, heads, row0):
    v = v_ref[...]
    rows, width = v.shape
    lane = lax.broadcasted_iota(jnp.int32, (rows, width), 1)
    head_masks = [(lane >= h * HEAD_DIM) & (lane < (h + 1) * HEAD_DIM) for h in range(heads)]

    def seg_mean(a):
        out = jnp.zeros_like(a)
        for m in head_masks:
            s = jnp.sum(jnp.where(m, a, 0.0), axis=-1, keepdims=True) * (1.0 / HEAD_DIM)
            out = jnp.where(m, s, out)
        return out

    d = v - seg_mean(v)
    var = seg_mean(d * d)
    vn = (d * lax.rsqrt(var + LN_EPS) * g_ref[...] + b_ref[...]).astype(BF16)

    r = lax.broadcasted_iota(jnp.int32, (GM_CHUNK, GM_CHUNK), 0)
    c = lax.broadcasted_iota(jnp.int32, (GM_CHUNK, GM_CHUNK), 1)
    tril = c <= r
    ws = [jnp.where(tril, ws_ref[h], 0.0).astype(BF16) for h in range(heads)]
    lane_c = lax.broadcasted_iota(jnp.int32, (GM_CHUNK, width), 1)
    chunk_masks = [(lane_c >= h * HEAD_DIM) & (lane_c < (h + 1) * HEAD_DIM) for h in range(heads)]
    bs = bs_ref[...]
    for j in range(rows // GM_CHUNK):
        r0 = j * GM_CHUNK
        vc = vn[r0:r0 + GM_CHUNK]
        z = jnp.zeros((GM_CHUNK, width), F32)
        for h in range(heads):
            zh = jnp.dot(ws[h], vc, preferred_element_type=F32)
            z = jnp.where(chunk_masks[h], zh, z)
        o_ref[0, row0 + r0:row0 + r0 + GM_CHUNK, :] = (
            u_ref[r0:r0 + GM_CHUNK, :] * (z + bs)).astype(BF16)


def _s5prep_kernel(lre_ref, lim_ref, ldt_ref, bre_ref, bim_ref, cre_ref, cim_ref,
                   lam_ref, laml_ref, bblk_ref, cblk_ref):
    lre = lre_ref[0]
    lim = lim_ref[0]
    dt = jnp.exp(ldt_ref[0])
    mag = jnp.exp(lre * dt)
    lbr = mag * jnp.cos(lim * dt)
    lbi = mag * jnp.sin(lim * dt)
    nr = lbr - 1.0
    den = lre * lre + lim * lim
    cr = (nr * lre + lbi * lim) / den
    ci = (lbi * lre - nr * lim) / den
    n = lre.shape[-1]
    lam_ref[0, :, 0:n] = lbr
    lam_ref[0, :, n:2 * n] = lbi
    pr, pi = lbr, lbi
    for _ in range(int(math.log2(S5_CHUNK))):
        pr, pi = pr * pr - pi * pi, 2.0 * (pr * pi)
    laml_ref[0, :, 0:n] = pr
    laml_ref[0, :, n:2 * n] = pi
    bre = bre_ref[0]
    bim = bim_ref[0]
    bblk_ref[0, :, 0:n] = (cr * bre - ci * bim).astype(BF16)
    bblk_ref[0, :, n:2 * n] = (cr * bim + ci * bre).astype(BF16)
    cblk_ref[0, 0:n, :] = cre_ref[0].astype(BF16)
    cblk_ref[0, n:2 * n, :] = (-cim_ref[0]).astype(BF16)


def _s5prep(lre, lim, ldt, bre_blk, bim_blk, cre_blk, cim_blk):
    depth, _, n = lre.shape
    w = bre_blk.shape[1]
    vec = pl.BlockSpec((1, 1, n), lambda l: (l, 0, 0))
    return pl.pallas_call(
        _s5prep_kernel,
        grid=(depth,),
        in_specs=[vec, vec, vec,
                  pl.BlockSpec((1, w, n), lambda l: (l, 0, 0)),
                  pl.BlockSpec((1, w, n), lambda l: (l, 0, 0)),
                  pl.BlockSpec((1, n, w), lambda l: (l, 0, 0)),
                  pl.BlockSpec((1, n, w), lambda l: (l, 0, 0))],
        out_specs=[pl.BlockSpec((1, 1, 2 * n), lambda l: (l, 0, 0)),
                   pl.BlockSpec((1, 1, 2 * n), lambda l: (l, 0, 0)),
                   pl.BlockSpec((1, w, 2 * n), lambda l: (l, 0, 0)),
                   pl.BlockSpec((1, 2 * n, w), lambda l: (l, 0, 0))],
        out_shape=[jax.ShapeDtypeStruct((depth, 1, 2 * n), F32),
                   jax.ShapeDtypeStruct((depth, 1, 2 * n), F32),
                   jax.ShapeDtypeStruct((depth, w, 2 * n), BF16),
                   jax.ShapeDtypeStruct((depth, 2 * n, w), BF16)],
        compiler_params=_cparams(1),
        name="s5prep",
    )(lre, lim, ldt, bre_blk, bim_blk, cre_blk, cim_blk)


def _s5_kernel(ua_ref, ub_ref, lam_ref, laml_ref, bblk_ref, cblk_ref, d_ref, wglu_ref, bglu_ref,
               o_ref, r_scr, xs_scr, e_scr, xin_scr, carry, y_scr, *, n):
    nc, lc = S5_NCHUNK, S5_CHUNK
    m_rows = nc * lc
    pair = ua_ref.shape[0]

    @pl.when(pl.program_id(1) == 0)
    def _():
        carry[...] = jnp.zeros_like(carry)

    def u_rows(s, r0, r1):
        return jnp.concatenate([ua_ref[s, r0:r1, :], ub_ref[s, r0:r1, :]], axis=1)

    lr = jnp.broadcast_to(lam_ref[:, 0:n], (nc, n))
    li = jnp.broadcast_to(lam_ref[:, n:2 * n], (nc, n))

    def expand(s):
        r_scr[s] = jnp.dot(u_rows(s, 0, m_rows).astype(BF16), bblk_ref[...],
                           preferred_element_type=F32)

    def scan(s, xr, xi, store):
        for t2 in range(lc // 2):
            r0 = t2 * 2 * nc
            kept = []
            for half in range(2):
                br = r_scr[s, r0 + half * nc:r0 + (half + 1) * nc, 0:n]
                bi = r_scr[s, r0 + half * nc:r0 + (half + 1) * nc, n:2 * n]
                xr, xi = lr * xr - li * xi + br, lr * xi + li * xr + bi
                kept.append((xr, xi))
            if store:
                xs_scr[s, r0:r0 + 2 * nc, 0:n] = jnp.concatenate(
                    [kept[0][0], kept[1][0]], axis=0).astype(BF16)
                xs_scr[s, r0:r0 + 2 * nc, n:2 * n] = jnp.concatenate(
                    [kept[0][1], kept[1][1]], axis=0).astype(BF16)
        return xr, xi

    def scans(s):
        zero = jnp.zeros((nc, n), F32)
        er, ei = scan(s, zero, zero, False)
        e_scr[s, :, 0:n] = er
        e_scr[s, :, n:2 * n] = ei
        llr = laml_ref[:, 0:n]
        lli = laml_ref[:, n:2 * n]
        cur_r = carry[s, :, 0:n]
        cur_i = carry[s, :, n:2 * n]
        for c in range(nc):
            xin_scr[s, c:c + 1, 0:n] = cur_r
            xin_scr[s, c:c + 1, n:2 * n] = cur_i
            e_r = e_scr[s, c:c + 1, 0:n]
            e_i = e_scr[s, c:c + 1, n:2 * n]
            cur_r, cur_i = llr * cur_r - lli * cur_i + e_r, llr * cur_i + lli * cur_r + e_i
        carry[s, :, 0:n] = cur_r
        carry[s, :, n:2 * n] = cur_i
        scan(s, xin_scr[s, :, 0:n], xin_scr[s, :, n:2 * n], True)

    def readout(s):
        step = 256
        for r0 in range(0, m_rows, step):
            y = jnp.dot(xs_scr[s, r0:r0 + step, :], cblk_ref[...], preferred_element_type=F32)
            y = y + d_ref[...] * u_rows(s, r0, r0 + step)
            y = _gelu_tanh(y)
            gate = (jnp.dot(y.astype(BF16), wglu_ref[...], preferred_element_type=F32)
                    + bglu_ref[...])
            out = y * jax.nn.sigmoid(gate)
            for hf in range(out.shape[1] // LANES):
                y_scr[s, hf, r0:r0 + step, :] = out[:, hf * LANES:(hf + 1) * LANES]
        for c in range(nc):
            for hf in range(y_scr.shape[1]):
                o_ref[s, c * lc:(c + 1) * lc, hf * LANES:(hf + 1) * LANES] = (
                    y_scr[s, hf, pl.ds(c, lc, stride=nc), :].astype(BF16))

    for s in range(pair):
        expand(s)
    for s in range(pair):
        scans(s)
        readout(s)


def _s5(u_a, u_b, lam, laml, bblk, cblk, d_row, wglu_blk, bglu_row):
    b, s, _ = u_a.shape
    w = 2 * LANES
    n = lam.shape[-1] // 2
    m_rows = S5_NCHUNK * S5_CHUNK
    pair = 2
    full = lambda a: pl.BlockSpec(a.shape, lambda i, j: (0,) * a.ndim)
    half = pl.BlockSpec((pair, m_rows, LANES), lambda i, j: (i, j, 0))
    return pl.pallas_call(
        functools.partial(_s5_kernel, n=n),
        grid=(b // pair, s // m_rows),
        in_specs=[half, half, full(lam), full(laml), full(bblk), full(cblk),
                  full(d_row), full(wglu_blk), full(bglu_row)],
        out_specs=pl.BlockSpec((pair, m_rows, w), lambda i, j: (i, j, 0)),
        out_shape=jax.ShapeDtypeStruct((b, s, w), BF16),
        scratch_shapes=[pltpu.VMEM((pair, m_rows, 2 * n), F32),
                        pltpu.VMEM((pair, m_rows, 2 * n), BF16),
                        pltpu.VMEM((pair, S5_NCHUNK, 2 * n), F32),
                        pltpu.VMEM((pair, S5_NCHUNK, 2 * n), F32),
                        pltpu.VMEM((pair, 1, 2 * n), F32),
                        pltpu.VMEM((pair, w // LANES, m_rows, LANES), F32)],
        compiler_params=_cparams(2),
        name="s5",
    )(u_a, u_b, lam, laml, bblk, cblk, d_row, wglu_blk, bglu_row)


def _fox_bias_tables(heads):
    sel = np.zeros((LANES, 2 * heads * LANES), np.float32)
    ones = np.zeros((1, 2 * heads * LANES), np.float32)
    koff = heads * LANES
    for h in range(heads):
        base = h * LANES + (HEAD_DIM if h % 2 == 0 else 0)
        for part in range(3):
            sel[part * heads + h, base + part] = 1.0
            ones[0, base + 3 + part] = 1.0
            ones[0, koff + base + part] = 1.0
            sel[part * heads + h, koff + base + 3 + part] = -1.0
    return jnp.asarray(sel, BF16), jnp.asarray(ones, F32)


def _foxprep_tile(fl, q_ref, k_ref, bf_ref, sel_ref, ones_ref, qa_ref, ka_ref, carry, cum_scr, heads,
                  row0):
    rows = fl.shape[0]
    x = fl + bf_ref[...]
    lane = lax.broadcasted_iota(jnp.int32, (rows, LANES), 1)
    ls = jnp.minimum(x, 0.0) - jnp.log1p(jnp.exp(-jnp.abs(x)))
    ls = jnp.where(lane < heads, ls, 0.0)

    r = lax.broadcasted_iota(jnp.int32, (LANES, LANES), 0)
    c = lax.broadcasted_iota(jnp.int32, (LANES, LANES), 1)
    ltri = jnp.where(c <= r, 1.0, 0.0).astype(BF16)
    run = carry[...]
    for j in range(rows // LANES):
        blk = ls[j * LANES:(j + 1) * LANES]
        parts = jnp.dot(ltri, jnp.concatenate(_split3(blk), axis=1), preferred_element_type=F32)
        loc = parts[:, 0:LANES] + parts[:, LANES:2 * LANES] + parts[:, 2 * LANES:3 * LANES]
        cum_blk = loc + run
        run = cum_blk[LANES - 1:LANES, :]
        cum_scr[j * LANES:(j + 1) * LANES, :] = cum_blk
    carry[...] = run

    chi, cmid, clo = _split3(cum_scr[...] * LOG2E)
    c24 = (chi.astype(F32) + pltpu.roll(cmid.astype(F32), heads, 1)
           + pltpu.roll(clo.astype(F32), 2 * heads, 1)).astype(BF16)
    bias = jnp.dot(c24, sel_ref[...], preferred_element_type=F32) + ones_ref[...]
    bias = bias.astype(BF16)
    koff = heads * LANES
    for h in range(heads):
        pair = (h // 2) * LANES
        keep = (lane < HEAD_DIM) if h % 2 == 0 else (lane >= HEAD_DIM)
        qa_ref[0, h, row0:row0 + rows, :] = jnp.where(
            keep, q_ref[:, pair:pair + LANES], bias[:, h * LANES:(h + 1) * LANES])
        ka_ref[0, h, row0:row0 + rows, :] = jnp.where(
            keep, k_ref[:, pair:pair + LANES], bias[:, koff + h * LANES:koff + (h + 1) * LANES])


def _premix_kernel(x_ref, mod_ref, w_ref, wf_ref, lng_ref, lnb_ref, ws_ref, bs_ref,
                   bf_ref, sel_ref, ones_ref,
                   ygm_ref, s5a_ref, s5b_ref, v_ref, qa_ref, ka_ref,
                   h_scr, gu_scr, gv_scr, q_scr, k_scr, carry, cum_scr,
                   *, gm, s5w, fw, gm_heads, fox_heads):
    @pl.when(pl.program_id(1) == 0)
    def _():
        carry[...] = jnp.zeros_like(carry)

    n_sub, tm = h_scr.shape[0], h_scr.shape[1]
    assert n_sub * tm == S5_NCHUNK * S5_CHUNK
    sh = mod_ref[0, 0:1, :]
    sc = mod_ref[0, 1:2, :]
    fls = []
    for sub in range(n_sub):
        row0 = sub * tm
        h_scr[sub] = (x_ref[0, row0:row0 + tm, :] * (1.0 + sc) + sh).astype(BF16)
        h = h_scr[sub]

        def proj(lo, hi):
            return jnp.dot(h, w_ref[:, lo:hi], preferred_element_type=F32)

        o = 0
        gu_scr[sub] = proj(o, o + gm); o += gm
        gv_scr[sub] = proj(o, o + gm); o += gm
        u = proj(o, o + s5w); o += s5w
        chunk0 = row0 // S5_CHUNK
        for cl in range(tm // S5_CHUNK):
            for hf, ref in enumerate((s5a_ref, s5b_ref)):
                ref[0, pl.ds(chunk0 + cl, S5_CHUNK, stride=S5_NCHUNK), :] = (
                    u[cl * S5_CHUNK:(cl + 1) * S5_CHUNK, hf * LANES:(hf + 1) * LANES])
        q_scr[sub] = (proj(o, o + fw) * (LOG2E * HEAD_DIM ** -0.5)).astype(BF16); o += fw
        k_scr[sub] = proj(o, o + fw).astype(BF16); o += fw
        v_ref[0, row0:row0 + tm, :] = proj(o, o + fw).astype(BF16)
        fls.append(jnp.dot(h, wf_ref[...], preferred_element_type=F32))

    for sub in range(n_sub):
        row0 = sub * tm
        _gmlp_tile(gu_scr.at[sub], gv_scr.at[sub], lng_ref, lnb_ref, ws_ref, bs_ref, ygm_ref,
                   gm_heads, row0)
        _foxprep_tile(fls[sub], q_scr.at[sub], k_scr.at[sub], bf_ref, sel_ref, ones_ref,
                      qa_ref, ka_ref, carry, cum_scr.at[sub], fox_heads, row0)


def _premix(x, mod, w_in, ln_g_row, ln_b_row, w_s, b_s_full, bf_row, sel, ones, layer,
            gm, s5w, fw):
    b, s, d = x.shape
    tm = ROW_TILE
    blk = S5_NCHUNK * S5_CHUNK
    n_sub = blk // tm
    n_main = 2 * gm + s5w + 3 * fw
    w_main_spec = _layer_spec(w_in, layer, (d, n_main))
    w_f_spec = pl.BlockSpec((None, d, LANES), lambda *_: (layer, 0, n_main // LANES))
    gm_heads = w_s.shape[1]
    fox_heads = fw // HEAD_DIM
    row = lambda n: pl.BlockSpec((1, blk, n), lambda i, j: (i, j, 0))
    full = lambda a: pl.BlockSpec(a.shape, lambda i, j: (0,) * a.ndim)
    aug = pl.BlockSpec((1, fox_heads, blk, LANES), lambda i, j: (i, 0, j, 0))
    lay = lambda a: _layer_spec(a, layer)
    assert s5w == 2 * LANES
    s5_half = row(LANES)
    outs = [jax.ShapeDtypeStruct((b, s, gm), BF16),
            jax.ShapeDtypeStruct((b, s, LANES), F32), jax.ShapeDtypeStruct((b, s, LANES), F32),
            jax.ShapeDtypeStruct((b, s, fw), BF16),
            jax.ShapeDtypeStruct((b, fox_heads, s, LANES), BF16),
            jax.ShapeDtypeStruct((b, fox_heads, s, LANES), BF16)]
    return pl.pallas_call(
        functools.partial(_premix_kernel, gm=gm, s5w=s5w, fw=fw, gm_heads=gm_heads,
                          fox_heads=fox_heads),
        grid=(b, s // blk),
        in_specs=[row(d), _mod_spec(mod, layer), w_main_spec, w_f_spec, lay(ln_g_row),
                  lay(ln_b_row), lay(w_s), lay(b_s_full), lay(bf_row), full(sel), full(ones)],
        out_specs=[row(gm), s5_half, s5_half, row(fw), aug, aug],
        out_shape=outs,
        scratch_shapes=[pltpu.VMEM((n_sub, tm, d), BF16),
                        pltpu.VMEM((n_sub, tm, gm), F32), pltpu.VMEM((n_sub, tm, gm), F32),
                        pltpu.VMEM((n_sub, tm, fw), BF16), pltpu.VMEM((n_sub, tm, fw), BF16),
                        pltpu.VMEM((1, LANES), F32), pltpu.VMEM((n_sub, tm, LANES), F32)],
        compiler_params=_cparams(2),
        name="premix",
    )(x, mod, w_in, w_in, ln_g_row, ln_b_row, w_s, b_s_full, bf_row, sel, ones)


def _fox_kernel(qa_ref, ka_ref, v_ref, o_ref, va_scr):
    nh = qa_ref.shape[1]
    s_len = qa_ref.shape[2]
    tq = FOX_BLOCK
    tk = tq
    nt = (((1,), (1,)), ((), ()))

    lane_v = lax.broadcasted_iota(jnp.int32, va_scr.shape[1:], 1)
    one = jnp.ones((), BF16)
    for hp in range(nh // 2):
        v2 = v_ref[0, :, hp * LANES:(hp + 1) * LANES]
        va_scr[2 * hp] = jnp.where(lane_v < HEAD_DIM, v2, one)
        va_scr[2 * hp + 1] = jnp.where(lane_v >= HEAD_DIM, v2, one)

    def step(q0, k0, width, state, mask_off):
        logits = [lax.dot_general(qa_ref[0, hh, q0:q0 + tq, :], ka_ref[0, hh, k0:k0 + width, :],
                                  nt, preferred_element_type=F32) for hh in range(nh)]
        new = []
        for hh in range(nh):
            m, acc = state[hh]
            s = logits[hh]
            if mask_off is not None:
                row = lax.broadcasted_iota(jnp.int32, (tq, width), 0)
                col = lax.broadcasted_iota(jnp.int32, (tq, width), 1)
                s = jnp.where(col <= row + mask_off, s, NEG_INF)
            m_new = jnp.maximum(m, jnp.max(s, axis=-1, keepdims=True))
            alpha = jnp.exp2(m - m_new)
            p = jnp.exp2(s - m_new)
            pv = jnp.dot(p.astype(BF16), va_scr[hh, k0:k0 + width, :],
                         preferred_element_type=F32)
            new.append((m_new, alpha * acc + pv))
        return tuple(new)

    wide = 2 * tk
    lane = lax.broadcasted_iota(jnp.int32, (tq, LANES), 1)
    for qi in range(s_len // tq):
        q0 = qi * tq
        state = tuple((jnp.full((tq, 1), -jnp.inf, F32), jnp.zeros((tq, LANES), F32))
                      for _ in range(nh))
        for j in range(qi // 2):
            state = step(q0, j * wide, wide, state, None)
        k_tail = (qi // 2) * wide
        if qi % 2 == 1:
            state = step(q0, k_tail, wide, state, tk)
        else:
            state = step(q0, k_tail, tk, state, 0)
        outs = [acc / pltpu.roll(acc, HEAD_DIM, 1) for _, acc in state]
        for hp in range(nh // 2):
            o_ref[0, q0:q0 + tq, hp * LANES:(hp + 1) * LANES] = jnp.where(
                lane < HEAD_DIM, outs[2 * hp], outs[2 * hp + 1]).astype(BF16)


def _fox(q_aug, k_aug, v):
    b, heads, s, _ = q_aug.shape
    fw = v.shape[-1]
    nh = FOX_HEADS_PER_STEP
    vw = nh // 2 * LANES
    seq = lambda: pl.BlockSpec((1, nh, s, LANES), lambda i, p: (i, p, 0, 0))
    return pl.pallas_call(
        _fox_kernel,
        grid=(b, heads // nh),
        in_specs=[seq(), seq(), pl.BlockSpec((1, s, vw), lambda i, p: (i, 0, p))],
        out_specs=pl.BlockSpec((1, s, vw), lambda i, p: (i, 0, p)),
        out_shape=jax.ShapeDtypeStruct((b, s, fw), BF16),
        scratch_shapes=[pltpu.VMEM((nh, s, LANES), BF16)],
        compiler_params=_cparams(2),
        name="fox",
    )(q_aug, k_aug, v)


def _mixffn_kernel(ygm_ref, ys5_ref, yfox_ref, wout_ref, x_ref, mod_ref, g1_ref, b1_ref,
                   wup_ref, cw_ref, cb_ref, wdn_ref, g2_ref, b2_ref, o_ref,
                   x1_scr, h_scr, halo, act_scr, *, alpha, gm, s5w, dff, chunk):
    @pl.when(pl.program_id(1) == 0)
    def _():
        halo[...] = jnp.zeros_like(halo)

    mix = jnp.dot(ygm_ref[0], wout_ref[0:gm, :], preferred_element_type=F32)
    mix += jnp.dot(ys5_ref[0], wout_ref[gm:gm + s5w, :], preferred_element_type=F32)
    mix += jnp.dot(yfox_ref[0], wout_ref[gm + s5w:, :], preferred_element_type=F32)
    gate1 = mod_ref[0, 2:3, :]
    x1_scr[...] = _layer_norm_rows(alpha * x_ref[0] + (1.0 + gate1) * mix, g1_ref[...], b1_ref[...])

    rows = x_ref.shape[1]
    sh = mod_ref[0, 3:4, :]
    sc = mod_ref[0, 4:5, :]
    gate = mod_ref[0, 5:6, :]
    h_scr[...] = (x1_scr[...] * (1.0 + sc) + sh).astype(BF16)
    h = h_scr[...]
    rid = lax.broadcasted_iota(jnp.int32, (rows, chunk), 0)
    for ci in range(dff // chunk):
        c0 = ci * chunk
        a = jnp.dot(h, wup_ref[:, c0:c0 + chunk], preferred_element_type=F32)
        gt = jnp.dot(h, wup_ref[:, dff + c0:dff + c0 + chunk], preferred_element_type=F32)
        p1 = halo[SUBLANES - 1:SUBLANES, c0:c0 + chunk]
        p2 = halo[SUBLANES - 2:SUBLANES - 1, c0:c0 + chunk]
        a1 = jnp.where(rid == 0, p1, pltpu.roll(a, 1, 0))
        a2 = jnp.where(rid == 0, p2, jnp.where(rid == 1, p1, pltpu.roll(a, 2, 0)))
        halo[:, c0:c0 + chunk] = a[rows - SUBLANES:rows, :]
        conv = (cb_ref[:, c0:c0 + chunk] + cw_ref[0:1, c0:c0 + chunk] * a2
                + cw_ref[1:2, c0:c0 + chunk] * a1 + cw_ref[2:3, c0:c0 + chunk] * a)
        act_scr[:, c0:c0 + chunk] = (_gelu_tanh(conv) * gt).astype(BF16)
    ffn = jnp.dot(act_scr[...], wdn_ref[...], preferred_element_type=F32)
    y = alpha * x1_scr[...] + (1.0 + gate) * ffn
    o_ref[0] = _layer_norm_rows(y, g2_ref[...], b2_ref[...])


def _mixffn(y_gm, y_s5, y_fox, w_out, x, mod, g1_row, b1_row,
            w_up, conv_w, conv_b_row, w_down, g2_row, b2_row, layer, alpha):
    b, s, d = x.shape
    gm, s5w, fw = y_gm.shape[-1], y_s5.shape[-1], y_fox.shape[-1]
    dff = w_down.shape[1]
    tm = FFN_ROW_TILE
    chunk = 256
    row = lambda n: pl.BlockSpec((1, tm, n), lambda i, j: (i, j, 0))
    full = lambda a: _layer_spec(a, layer)
    resident = lambda a: pl.BlockSpec((None,) + a.shape[1:], lambda i, j: (layer,) + (0,) * (a.ndim - 1),
                                      pipeline_mode=pl.Buffered(1))
    return pl.pallas_call(
        functools.partial(_mixffn_kernel, alpha=alpha, gm=gm, s5w=s5w, dff=dff, chunk=chunk),
        grid=(b, s // tm),
        in_specs=[row(gm), row(s5w), row(fw), resident(w_out), row(d),
                  _mod_spec(mod, layer), full(g1_row), full(b1_row),
                  resident(w_up), full(conv_w), full(conv_b_row), resident(w_down),
                  full(g2_row), full(b2_row)],
        out_specs=row(d),
        out_shape=jax.ShapeDtypeStruct((b, s, d), F32),
        scratch_shapes=[pltpu.VMEM((tm, d), F32),
                        pltpu.VMEM((tm, d), BF16),
                        pltpu.VMEM((SUBLANES, dff), F32),
                        pltpu.VMEM((tm, dff), BF16)],
        compiler_params=_cparams(2),
        name="mixffn",
    )(y_gm, y_s5, y_fox, w_out, x, mod, g1_row, b1_row,
      w_up, conv_w, conv_b_row, w_down, g2_row, b2_row)


def _block_diag(a):
    depth, g, r, c = a.shape
    eye = jnp.eye(g, dtype=a.dtype)
    return (a[:, :, :, None, :] * eye[None, :, None, :, None]).reshape(depth, g * r, g * c)


def kernel(x, c, w_ada, b_ada, w_in, b_f, gm_ln_g, gm_ln_b, gm_w_s, gm_b_s, s5_lam_re, s5_lam_im, s5_log_dt, s5_b_re, s5_b_im, s5_c_re, s5_c_im, s5_d, s5_w_glu, s5_b_glu, w_out, ln1_g, ln1_b, w_up, conv_w, conv_b, w_down, ln2_g, ln2_b):
    depth, d, _ = w_in.shape
    batch = x.shape[0]
    gm_heads = gm_w_s.shape[1]
    gm = gm_heads * HEAD_DIM
    groups = s5_lam_re.shape[1]
    s5w = groups * S5_GROUP_DIM
    fox_heads = b_f.shape[1]
    fw = fox_heads * HEAD_DIM
    n_main = 2 * gm + s5w + 3 * fw
    alpha = (2.0 * depth) ** 0.25

    pad_rows = 2 * SUBLANES
    c_pad = jnp.pad(c, ((0, pad_rows - batch), (0, 0)))
    mod = _adaln(c_pad, w_ada, b_ada[:, None, :])[:, :batch]
    mod = mod.reshape(depth, batch, 6, d)

    n_state = groups * S5_STATE
    lre = s5_lam_re.reshape(depth, 1, n_state)
    lim = s5_lam_im.reshape(depth, 1, n_state)
    ldt = jnp.repeat(s5_log_dt, S5_STATE, axis=-1).reshape(depth, 1, n_state)
    bre_blk = _block_diag(jnp.swapaxes(s5_b_re, -1, -2))
    bim_blk = _block_diag(jnp.swapaxes(s5_b_im, -1, -2))
    cre_blk = _block_diag(jnp.swapaxes(s5_c_re, -1, -2))
    cim_blk = _block_diag(jnp.swapaxes(s5_c_im, -1, -2))
    lam, laml, bblk, cblk = _s5prep(lre, lim, ldt, bre_blk, bim_blk, cre_blk, cim_blk)
    wglu_blk = _block_diag(jnp.swapaxes(s5_w_glu, -1, -2)).astype(BF16)

    w_in_bf = jnp.pad(w_in, ((0, 0), (0, 0), (0, LANES - fox_heads))).astype(BF16)
    w_out_bf = w_out.astype(BF16)
    w_up_bf = w_up.astype(BF16)
    w_down_bf = w_down.astype(BF16)
    row3 = lambda a: a.reshape(depth, 1, -1)
    gm_g, gm_b = row3(gm_ln_g), row3(gm_ln_b)
    bs_full = jnp.repeat(jnp.swapaxes(gm_b_s, 1, 2), HEAD_DIM, axis=2)
    s5_d_row, s5_bglu_row = row3(s5_d), row3(s5_b_glu)
    bf_row = jnp.pad(b_f[:, None, :], ((0, 0), (0, 0), (0, LANES - fox_heads)))
    sel, ones = _fox_bias_tables(fox_heads)
    ln1_g3, ln1_b3, ln2_g3, ln2_b3 = row3(ln1_g), row3(ln1_b), row3(ln2_g), row3(ln2_b)
    conv_b3 = row3(conv_b)

    for l in range(depth):
        y_gm, s5_a, s5_b, v, q_aug, k_aug = _premix(x, mod, w_in_bf, gm_g, gm_b, gm_w_s, bs_full,
                                                    bf_row, sel, ones, l, gm, s5w, fw)
        y_s5 = _s5(s5_a, s5_b, lam[l], laml[l], bblk[l], cblk[l], s5_d_row[l], wglu_blk[l],
                   s5_bglu_row[l])
        y_fox = _fox(q_aug, k_aug, v)
        x = _mixffn(y_gm, y_s5, y_fox, w_out_bf, x, mod, ln1_g3, ln1_b3,
                    w_up_bf, conv_w, conv_b3, w_down_bf, ln2_g3, ln2_b3, l, alpha)
    return x
```

```python
import functools
import math

import jax
import jax.numpy as jnp
from jax import lax
from jax.experimental import pallas as pl
from jax.experimental.pallas import tpu as pltpu

F32 = jnp.float32
BF16 = jnp.bfloat16

HEAD_DIM = 64
GM_CHUNK = 128
S5_GROUP_DIM = 16
S5_STATE = 64
CONV_WIDTH = 3
LN_EPS = 1e-5
NEG_INF = -1e30
LOG2E = math.log2(math.e)

LANES = 128
SUBLANES = 8
ROW_TILE = 512
FFN_ROW_TILE = 512
S5_CHUNK = 128
S5_NCHUNK = SUBLANES
FOX_BLOCK = 512
FOX_HEADS_PER_STEP = 2
VMEM_LIMIT = 56 * 1024 * 1024


def _cparams(n_axes, vmem=VMEM_LIMIT, flags=None):
    return pltpu.CompilerParams(dimension_semantics=("arbitrary",) * n_axes,
                                vmem_limit_bytes=vmem, flags=flags)


def _layer_norm_rows(y, g, b):
    mu = jnp.mean(y, axis=-1, keepdims=True)
    d = y - mu
    var = jnp.mean(d * d, axis=-1, keepdims=True)
    return d * lax.rsqrt(var + LN_EPS) * g + b


def _gelu_tanh(x):
    c = math.sqrt(2.0 / math.pi)
    return 0.5 * x * (1.0 + jnp.tanh(c * (x + 0.044715 * (x * x * x))))


def _split3(x):
    hi = x.astype(BF16)
    r1 = x - hi.astype(F32)
    mid = r1.astype(BF16)
    lo = (r1 - mid.astype(F32)).astype(BF16)
    return hi, mid, lo


def _adaln_kernel(c_ref, w_ref, b_ref, o_ref):
    c = c_ref[...]
    cond = (c * jax.nn.sigmoid(c)).astype(BF16)
    w = w_ref[0].astype(BF16)
    o_ref[0] = jnp.dot(cond, w, preferred_element_type=F32) + b_ref[0]


def _adaln(c_pad, w_ada, b_ada3):
    depth, d, n = w_ada.shape
    rows = c_pad.shape[0]
    tn = 1536
    return pl.pallas_call(
        _adaln_kernel,
        grid=(depth, n // tn),
        in_specs=[pl.BlockSpec((rows, d), lambda l, j: (0, 0)),
                  pl.BlockSpec((1, d, tn), lambda l, j: (l, 0, j)),
                  pl.BlockSpec((1, 1, tn), lambda l, j: (l, 0, j))],
        out_specs=pl.BlockSpec((1, rows, tn), lambda l, j: (l, 0, j)),
        out_shape=jax.ShapeDtypeStruct((depth, rows, n), F32),
        compiler_params=_cparams(2),
        name="adaln",
    )(c_pad, w_ada, b_ada3)


def _layer_spec(a, layer, block=None):
    shape = tuple(a.shape[1:]) if block is None else tuple(block)
    return pl.BlockSpec((None,) + shape, lambda *_: (layer,) + (0,) * len(shape))


def _mod_spec(mod, layer):
    return pl.BlockSpec((None, 1) + tuple(mod.shape[2:]), lambda i, *_: (layer, i, 0, 0))


def _gmlp_tile(u_ref, v_ref, g_ref, b_ref, ws_ref, bs_ref, o_ref, heads, row0):
    v = v_ref[...]
    rows, width = v.shape
    lane = lax.broadcasted_iota(jnp.int32, (rows, width), 1)
    head_masks = [(lane >= h * HEAD_DIM) & (lane < (h + 1) * HEAD_DIM) for h in range(heads)]

    def seg_mean(a):
        out = jnp.zeros_like(a)
        for m in head_masks:
            s = jnp.sum(jnp.where(m, a, 0.0), axis=-1, keepdims=True) * (1.0 / HEAD_DIM)
            out = jnp.where(m, s, out)
        return out

    d = v - seg_mean(v)
    var = seg_mean(d * d)
    vn = (d * lax.rsqrt(var + LN_EPS) * g_ref[...] + b_ref[...]).astype(BF16)

    r = lax.broadcasted_iota(jnp.int32, (GM_CHUNK, GM_CHUNK), 0)
    c = lax.broadcasted_iota(jnp.int32, (GM_CHUNK, GM_CHUNK), 1)
    tril = c <= r
    w_cat = jnp.concatenate([jnp.where(tril, ws_ref[h], 0.0).astype(BF16) for h in range(heads)],
                            axis=1)
    lane_c = lax.broadcasted_iota(jnp.int32, (GM_CHUNK, width), 1)
    chunk_masks = [(lane_c >= h * HEAD_DIM) & (lane_c < (h + 1) * HEAD_DIM) for h in range(heads)]
    bs = bs_ref[...]
    zero = jnp.zeros((), BF16)
    for j in range(rows // GM_CHUNK):
        r0 = j * GM_CHUNK
        vc = vn[r0:r0 + GM_CHUNK]
        stacked = jnp.concatenate([jnp.where(chunk_masks[h], vc, zero) for h in range(heads)],
                                  axis=0)
        z = jnp.dot(w_cat, stacked, preferred_element_type=F32)
        o_ref[0, row0 + r0:row0 + r0 + GM_CHUNK, :] = (
            u_ref[r0:r0 + GM_CHUNK, :] * (z + bs)).astype(BF16)


def _s5prep_kernel(lre_ref, lim_ref, ldt_ref, bre_ref, bim_ref, cre_ref, cim_ref,
                   lam_ref, laml_ref, bblk_ref, cblk_ref):
    lre = lre_ref[0]
    lim = lim_ref[0]
    dt = jnp.exp(ldt_ref[0])
    mag = jnp.exp(lre * dt)
    lbr = mag * jnp.cos(lim * dt)
    lbi = mag * jnp.sin(lim * dt)
    nr = lbr - 1.0
    den = lre * lre + lim * lim
    cr = (nr * lre + lbi * lim) / den
    ci = (lbi * lre - nr * lim) / den
    n = lre.shape[-1]
    lam_ref[0, :, 0:n] = lbr
    lam_ref[0, :, n:2 * n] = lbi
    pr, pi = lbr, lbi
    for _ in range(int(math.log2(S5_CHUNK))):
        pr, pi = pr * pr - pi * pi, 2.0 * (pr * pi)
    laml_ref[0, :, 0:n] = pr
    laml_ref[0, :, n:2 * n] = pi
    bre = bre_ref[0]
    bim = bim_ref[0]
    bblk_ref[0, :, 0:n] = (cr * bre - ci * bim).astype(BF16)
    bblk_ref[0, :, n:2 * n] = (cr * bim + ci * bre).astype(BF16)
    cblk_ref[0, 0:n, :] = cre_ref[0].astype(BF16)
    cblk_ref[0, n:2 * n, :] = (-cim_ref[0]).astype(BF16)


def _s5prep(lre, lim, ldt, bre_blk, bim_blk, cre_blk, cim_blk):
    depth, _, n = lre.shape
    w = bre_blk.shape[1]
    vec = pl.BlockSpec((1, 1, n), lambda l: (l, 0, 0))
    return pl.pallas_call(
        _s5prep_kernel,
        grid=(depth,),
        in_specs=[vec, vec, vec,
                  pl.BlockSpec((1, w, n), lambda l: (l, 0, 0)),
                  pl.BlockSpec((1, w, n), lambda l: (l, 0, 0)),
                  pl.BlockSpec((1, n, w), lambda l: (l, 0, 0)),
                  pl.BlockSpec((1, n, w), lambda l: (l, 0, 0))],
        out_specs=[pl.BlockSpec((1, 1, 2 * n), lambda l: (l, 0, 0)),
                   pl.BlockSpec((1, 1, 2 * n), lambda l: (l, 0, 0)),
                   pl.BlockSpec((1, w, 2 * n), lambda l: (l, 0, 0)),
                   pl.BlockSpec((1, 2 * n, w), lambda l: (l, 0, 0))],
        out_shape=[jax.ShapeDtypeStruct((depth, 1, 2 * n), F32),
                   jax.ShapeDtypeStruct((depth, 1, 2 * n), F32),
                   jax.ShapeDtypeStruct((depth, w, 2 * n), BF16),
                   jax.ShapeDtypeStruct((depth, 2 * n, w), BF16)],
        compiler_params=_cparams(1),
        name="s5prep",
    )(lre, lim, ldt, bre_blk, bim_blk, cre_blk, cim_blk)


def _s5_kernel(ua_ref, ub_ref, lam_ref, laml_ref, bblk_ref, cblk_ref, d_ref, wglu_ref, bglu_ref,
               o_ref, r_scr, xs_scr, e_scr, xin_scr, carry, y_scr, *, n):
    nc, lc = S5_NCHUNK, S5_CHUNK
    m_rows = nc * lc
    pair = ua_ref.shape[0]

    @pl.when(pl.program_id(1) == 0)
    def _():
        carry[...] = jnp.zeros_like(carry)

    def u_rows(s, r0, r1):
        return jnp.concatenate([ua_ref[s, r0:r1, :], ub_ref[s, r0:r1, :]], axis=1)

    lr = jnp.broadcast_to(lam_ref[:, 0:n], (nc, n))
    li = jnp.broadcast_to(lam_ref[:, n:2 * n], (nc, n))

    def expand(s):
        r_scr[s] = jnp.dot(u_rows(s, 0, m_rows).astype(BF16), bblk_ref[...],
                           preferred_element_type=F32)

    def scan(s, xr, xi, store):
        for t2 in range(lc // 2):
            r0 = t2 * 2 * nc
            kept = []
            for half in range(2):
                br = r_scr[s, r0 + half * nc:r0 + (half + 1) * nc, 0:n]
                bi = r_scr[s, r0 + half * nc:r0 + (half + 1) * nc, n:2 * n]
                xr, xi = lr * xr - li * xi + br, lr * xi + li * xr + bi
                kept.append((xr, xi))
            if store:
                xs_scr[s, r0:r0 + 2 * nc, 0:n] = jnp.concatenate(
                    [kept[0][0], kept[1][0]], axis=0).astype(BF16)
                xs_scr[s, r0:r0 + 2 * nc, n:2 * n] = jnp.concatenate(
                    [kept[0][1], kept[1][1]], axis=0).astype(BF16)
        return xr, xi

    def scans(s):
        zero = jnp.zeros((nc, n), F32)
        er, ei = scan(s, zero, zero, False)
        e_scr[s, :, 0:n] = er
        e_scr[s, :, n:2 * n] = ei
        llr = laml_ref[:, 0:n]
        lli = laml_ref[:, n:2 * n]
        cur_r = carry[s, :, 0:n]
        cur_i = carry[s, :, n:2 * n]
        for c in range(nc):
            xin_scr[s, c:c + 1, 0:n] = cur_r
            xin_scr[s, c:c + 1, n:2 * n] = cur_i
            e_r = e_scr[s, c:c + 1, 0:n]
            e_i = e_scr[s, c:c + 1, n:2 * n]
            cur_r, cur_i = llr * cur_r - lli * cur_i + e_r, llr * cur_i + lli * cur_r + e_i
        carry[s, :, 0:n] = cur_r
        carry[s, :, n:2 * n] = cur_i
        scan(s, xin_scr[s, :, 0:n], xin_scr[s, :, n:2 * n], True)

    def readout(s):
        step = 256
        for r0 in range(0, m_rows, step):
            y = jnp.dot(xs_scr[s, r0:r0 + step, :], cblk_ref[...], preferred_element_type=F32)
            y = y + d_ref[...] * u_rows(s, r0, r0 + step)
            y = _gelu_tanh(y)
            gate = (jnp.dot(y.astype(BF16), wglu_ref[...], preferred_element_type=F32)
                    + bglu_ref[...])
            out = y * jax.nn.sigmoid(gate)
            for hf in range(out.shape[1] // LANES):
                y_scr[s, hf, r0:r0 + step, :] = out[:, hf * LANES:(hf + 1) * LANES]
        for c in range(nc):
            for hf in range(y_scr.shape[1]):
                o_ref[s, c * lc:(c + 1) * lc, hf * LANES:(hf + 1) * LANES] = (
                    y_scr[s, hf, pl.ds(c, lc, stride=nc), :].astype(BF16))

    for s in range(pair):
        expand(s)
    for s in range(pair):
        scans(s)
        readout(s)


def _s5(u_a, u_b, lam, laml, bblk, cblk, d_row, wglu_blk, bglu_row):
    b, s, _ = u_a.shape
    w = 2 * LANES
    n = lam.shape[-1] // 2
    m_rows = S5_NCHUNK * S5_CHUNK
    pair = 2
    full = lambda a: pl.BlockSpec(a.shape, lambda i, j: (0,) * a.ndim)
    half = pl.BlockSpec((pair, m_rows, LANES), lambda i, j: (i, j, 0))
    return pl.pallas_call(
        functools.partial(_s5_kernel, n=n),
        grid=(b // pair, s // m_rows),
        in_specs=[half, half, full(lam), full(laml), full(bblk), full(cblk),
                  full(d_row), full(wglu_blk), full(bglu_row)],
        out_specs=pl.BlockSpec((pair, m_rows, w), lambda i, j: (i, j, 0)),
        out_shape=jax.ShapeDtypeStruct((b, s, w), BF16),
        scratch_shapes=[pltpu.VMEM((pair, m_rows, 2 * n), F32),
                        pltpu.VMEM((pair, m_rows, 2 * n), BF16),
                        pltpu.VMEM((pair, S5_NCHUNK, 2 * n), F32),
                        pltpu.VMEM((pair, S5_NCHUNK, 2 * n), F32),
                        pltpu.VMEM((pair, 1, 2 * n), F32),
                        pltpu.VMEM((pair, w // LANES, m_rows, LANES), F32)],
        compiler_params=_cparams(2),
        name="s5",
    )(u_a, u_b, lam, laml, bblk, cblk, d_row, wglu_blk, bglu_row)


def _foxprep_tile(fl, q_ref, k_ref, bf_ref, qa_ref, ka_ref, carry, cum_scr, heads, row0):
    rows = fl.shape[0]
    x = fl + bf_ref[...]
    lane = lax.broadcasted_iota(jnp.int32, (rows, LANES), 1)
    ls = jnp.minimum(x, 0.0) - jnp.log1p(jnp.exp(-jnp.abs(x)))
    ls = jnp.where(lane < heads, ls, 0.0)

    r = lax.broadcasted_iota(jnp.int32, (LANES, LANES), 0)
    c = lax.broadcasted_iota(jnp.int32, (LANES, LANES), 1)
    ltri = jnp.where(c <= r, 1.0, 0.0).astype(BF16)
    run = carry[...]
    for j in range(rows // LANES):
        blk = ls[j * LANES:(j + 1) * LANES]
        parts = jnp.dot(ltri, jnp.concatenate(_split3(blk), axis=1), preferred_element_type=F32)
        loc = parts[:, 0:LANES] + parts[:, LANES:2 * LANES] + parts[:, 2 * LANES:3 * LANES]
        cum_blk = loc + run
        run = cum_blk[LANES - 1:LANES, :]
        cum_scr[j * LANES:(j + 1) * LANES, :] = cum_blk
    carry[...] = run

    chi, cmid, clo = _split3(cum_scr[...] * LOG2E)
    packed = (chi.astype(F32) + pltpu.roll(cmid.astype(F32), heads, 1)
              + pltpu.roll(clo.astype(F32), 2 * heads, 1))
    lane1 = lax.broadcasted_iota(jnp.int32, (1, LANES), 1)
    for h in range(heads):
        pair = (h // 2) * LANES
        keep = (lane < HEAD_DIM) if h % 2 == 0 else (lane >= HEAD_DIM)
        base = HEAD_DIM if h % 2 == 0 else 0
        lanes_of = lambda first: sum(
            jnp.where(lane1 == base + (first + p) * heads + h, 1.0, 0.0) for p in range(3))
        part_lanes, ones_lanes = lanes_of(0), lanes_of(3)
        at_parts = packed if base == 0 else pltpu.roll(packed, base, 1)
        at_ones = pltpu.roll(packed, base + 3 * heads, 1)
        q_bias = (at_parts * part_lanes + ones_lanes).astype(BF16)
        k_bias = (part_lanes - at_ones * ones_lanes).astype(BF16)
        qa_ref[0, h, row0:row0 + rows, :] = jnp.where(keep, q_ref[:, pair:pair + LANES], q_bias)
        ka_ref[0, h, row0:row0 + rows, :] = jnp.where(keep, k_ref[:, pair:pair + LANES], k_bias)


def _premix_kernel(x_ref, mod_ref, w_ref, wf_ref, lng_ref, lnb_ref, ws_ref, bs_ref, bf_ref,
                   ygm_ref, s5a_ref, s5b_ref, v_ref, qa_ref, ka_ref,
                   h_scr, gu_scr, gv_scr, q_scr, k_scr, carry, cum_scr,
                   *, gm, s5w, fw, gm_heads, fox_heads):
    @pl.when(pl.program_id(1) == 0)
    def _():
        carry[...] = jnp.zeros_like(carry)

    n_sub, tm = h_scr.shape[0], h_scr.shape[1]
    assert n_sub * tm == S5_NCHUNK * S5_CHUNK
    sh = mod_ref[0, 0:1, :]
    sc = mod_ref[0, 1:2, :]
    fls = []
    for sub in range(n_sub):
        row0 = sub * tm
        h_scr[sub] = (x_ref[0, row0:row0 + tm, :] * (1.0 + sc) + sh).astype(BF16)
        h = h_scr[sub]

        def proj(lo, hi):
            return jnp.dot(h, w_ref[:, lo:hi], preferred_element_type=F32)

        o = 0
        gu_scr[sub] = proj(o, o + gm); o += gm
        gv_scr[sub] = proj(o, o + gm); o += gm
        u = proj(o, o + s5w); o += s5w
        chunk0 = row0 // S5_CHUNK
        for cl in range(tm // S5_CHUNK):
            for hf, ref in enumerate((s5a_ref, s5b_ref)):
                ref[0, pl.ds(chunk0 + cl, S5_CHUNK, stride=S5_NCHUNK), :] = (
                    u[cl * S5_CHUNK:(cl + 1) * S5_CHUNK, hf * LANES:(hf + 1) * LANES])
        q_scr[sub] = (proj(o, o + fw) * (LOG2E * HEAD_DIM ** -0.5)).astype(BF16); o += fw
        k_scr[sub] = proj(o, o + fw).astype(BF16); o += fw
        v_ref[0, row0:row0 + tm, :] = proj(o, o + fw).astype(BF16)
        fls.append(jnp.dot(h, wf_ref[...], preferred_element_type=F32))

    for sub in range(n_sub):
        row0 = sub * tm
        _gmlp_tile(gu_scr.at[sub], gv_scr.at[sub], lng_ref, lnb_ref, ws_ref, bs_ref, ygm_ref,
                   gm_heads, row0)
        _foxprep_tile(fls[sub], q_scr.at[sub], k_scr.at[sub], bf_ref, qa_ref, ka_ref, carry,
                      cum_scr.at[sub], fox_heads, row0)


def _premix(x, mod, w_in, ln_g_row, ln_b_row, w_s, b_s_full, bf_row, layer, gm, s5w, fw):
    b, s, d = x.shape
    tm = ROW_TILE
    blk = S5_NCHUNK * S5_CHUNK
    n_sub = blk // tm
    n_main = 2 * gm + s5w + 3 * fw
    w_main_spec = _layer_spec(w_in, layer, (d, n_main))
    w_f_spec = pl.BlockSpec((None, d, LANES), lambda *_: (layer, 0, n_main // LANES))
    gm_heads = w_s.shape[1]
    fox_heads = fw // HEAD_DIM
    row = lambda n: pl.BlockSpec((1, blk, n), lambda i, j: (i, j, 0))
    full = lambda a: pl.BlockSpec(a.shape, lambda i, j: (0,) * a.ndim)
    aug = pl.BlockSpec((1, fox_heads, blk, LANES), lambda i, j: (i, 0, j, 0))
    lay = lambda a: _layer_spec(a, layer)
    assert s5w == 2 * LANES
    s5_half = row(LANES)
    outs = [jax.ShapeDtypeStruct((b, s, gm), BF16),
            jax.ShapeDtypeStruct((b, s, LANES), F32), jax.ShapeDtypeStruct((b, s, LANES), F32),
            jax.ShapeDtypeStruct((b, s, fw), BF16),
            jax.ShapeDtypeStruct((b, fox_heads, s, LANES), BF16),
            jax.ShapeDtypeStruct((b, fox_heads, s, LANES), BF16)]
    return pl.pallas_call(
        functools.partial(_premix_kernel, gm=gm, s5w=s5w, fw=fw, gm_heads=gm_heads,
                          fox_heads=fox_heads),
        grid=(b, s // blk),
        in_specs=[row(d), _mod_spec(mod, layer), w_main_spec, w_f_spec, lay(ln_g_row),
                  lay(ln_b_row), lay(w_s), lay(b_s_full), lay(bf_row)],
        out_specs=[row(gm), s5_half, s5_half, row(fw), aug, aug],
        out_shape=outs,
        scratch_shapes=[pltpu.VMEM((n_sub, tm, d), BF16),
                        pltpu.VMEM((n_sub, tm, gm), F32), pltpu.VMEM((n_sub, tm, gm), F32),
                        pltpu.VMEM((n_sub, tm, fw), BF16), pltpu.VMEM((n_sub, tm, fw), BF16),
                        pltpu.VMEM((1, LANES), F32), pltpu.VMEM((n_sub, tm, LANES), F32)],
        compiler_params=_cparams(2),
        name="premix",
    )(x, mod, w_in, w_in, ln_g_row, ln_b_row, w_s, b_s_full, bf_row)


def _fox_kernel(qa_ref, ka_ref, v_ref, o_ref, va_scr):
    nh = qa_ref.shape[1]
    s_len = qa_ref.shape[2]
    tq = FOX_BLOCK
    tk = tq
    nt = (((1,), (1,)), ((), ()))

    lane_v = lax.broadcasted_iota(jnp.int32, va_scr.shape[1:], 1)
    one = jnp.ones((), BF16)
    for hp in range(nh // 2):
        v2 = v_ref[0, :, hp * LANES:(hp + 1) * LANES]
        va_scr[2 * hp] = jnp.where(lane_v < HEAD_DIM, v2, one)
        va_scr[2 * hp + 1] = jnp.where(lane_v >= HEAD_DIM, v2, one)

    def step(q0, k0, width, state, mask_off):
        logits = [lax.dot_general(qa_ref[0, hh, q0:q0 + tq, :], ka_ref[0, hh, k0:k0 + width, :],
                                  nt, preferred_element_type=F32) for hh in range(nh)]
        new = []
        for hh in range(nh):
            m, acc = state[hh]
            s = logits[hh]
            if mask_off is not None:
                row = lax.broadcasted_iota(jnp.int32, (tq, width), 0)
                col = lax.broadcasted_iota(jnp.int32, (tq, width), 1)
                s = jnp.where(col <= row + mask_off, s, NEG_INF)
            m_new = jnp.maximum(m, jnp.max(s, axis=-1, keepdims=True))
            alpha = jnp.exp2(m - m_new)
            p = jnp.exp2(s - m_new)
            pv = jnp.dot(p.astype(BF16), va_scr[hh, k0:k0 + width, :],
                         preferred_element_type=F32)
            new.append((m_new, alpha * acc + pv))
        return tuple(new)

    wide = 2 * tk
    lane = lax.broadcasted_iota(jnp.int32, (tq, LANES), 1)
    for qi in range(s_len // tq):
        q0 = qi * tq
        state = tuple((jnp.full((tq, 1), -jnp.inf, F32), jnp.zeros((tq, LANES), F32))
                      for _ in range(nh))
        for j in range(qi // 2):
            state = step(q0, j * wide, wide, state, None)
        k_tail = (qi // 2) * wide
        if qi % 2 == 1:
            state = step(q0, k_tail, wide, state, tk)
        else:
            state = step(q0, k_tail, tk, state, 0)
        outs = [acc / pltpu.roll(acc, HEAD_DIM, 1) for _, acc in state]
        for hp in range(nh // 2):
            o_ref[0, q0:q0 + tq, hp * LANES:(hp + 1) * LANES] = jnp.where(
                lane < HEAD_DIM, outs[2 * hp], outs[2 * hp + 1]).astype(BF16)


def _fox(q_aug, k_aug, v):
    b, heads, s, _ = q_aug.shape
    fw = v.shape[-1]
    nh = FOX_HEADS_PER_STEP
    vw = nh // 2 * LANES
    seq = lambda: pl.BlockSpec((1, nh, s, LANES), lambda i, p: (i, p, 0, 0))
    return pl.pallas_call(
        _fox_kernel,
        grid=(b, heads // nh),
        in_specs=[seq(), seq(), pl.BlockSpec((1, s, vw), lambda i, p: (i, 0, p))],
        out_specs=pl.BlockSpec((1, s, vw), lambda i, p: (i, 0, p)),
        out_shape=jax.ShapeDtypeStruct((b, s, fw), BF16),
        scratch_shapes=[pltpu.VMEM((nh, s, LANES), BF16)],
        compiler_params=_cparams(2),
        name="fox",
    )(q_aug, k_aug, v)


def _mixffn_kernel(ygm_ref, ys5_ref, yfox_ref, wout_ref, x_ref, mod_ref, g1_ref, b1_ref,
                   wup_ref, cw_ref, cb_ref, wdn_ref, g2_ref, b2_ref, o_ref,
                   x1_scr, h_scr, halo, act_scr, *, alpha, gm, s5w, dff, chunk):
    @pl.when(pl.program_id(1) == 0)
    def _():
        halo[...] = jnp.zeros_like(halo)

    mix = jnp.dot(ygm_ref[0], wout_ref[0:gm, :], preferred_element_type=F32)
    mix += jnp.dot(ys5_ref[0], wout_ref[gm:gm + s5w, :], preferred_element_type=F32)
    mix += jnp.dot(yfox_ref[0], wout_ref[gm + s5w:, :], preferred_element_type=F32)
    gate1 = mod_ref[0, 2:3, :]
    x1_scr[...] = _layer_norm_rows(alpha * x_ref[0] + (1.0 + gate1) * mix, g1_ref[...], b1_ref[...])

    rows = x_ref.shape[1]
    sh = mod_ref[0, 3:4, :]
    sc = mod_ref[0, 4:5, :]
    gate = mod_ref[0, 5:6, :]
    h_scr[...] = (x1_scr[...] * (1.0 + sc) + sh).astype(BF16)
    h = h_scr[...]
    rid = lax.broadcasted_iota(jnp.int32, (rows, chunk), 0)
    for ci in range(dff // chunk):
        c0 = ci * chunk
        a = jnp.dot(h, wup_ref[:, c0:c0 + chunk], preferred_element_type=F32)
        gt = jnp.dot(h, wup_ref[:, dff + c0:dff + c0 + chunk], preferred_element_type=F32)
        p1 = halo[SUBLANES - 1:SUBLANES, c0:c0 + chunk]
        p2 = halo[SUBLANES - 2:SUBLANES - 1, c0:c0 + chunk]
        a1 = jnp.where(rid == 0, p1, pltpu.roll(a, 1, 0))
        a2 = jnp.where(rid == 0, p2, jnp.where(rid == 1, p1, pltpu.roll(a, 2, 0)))
        halo[:, c0:c0 + chunk] = a[rows - SUBLANES:rows, :]
        conv = (cb_ref[:, c0:c0 + chunk] + cw_ref[0:1, c0:c0 + chunk] * a2
                + cw_ref[1:2, c0:c0 + chunk] * a1 + cw_ref[2:3, c0:c0 + chunk] * a)
        act_scr[:, c0:c0 + chunk] = (_gelu_tanh(conv) * gt).astype(BF16)
    ffn = jnp.dot(act_scr[...], wdn_ref[...], preferred_element_type=F32)
    y = alpha * x1_scr[...] + (1.0 + gate) * ffn
    o_ref[0] = _layer_norm_rows(y, g2_ref[...], b2_ref[...])


def _mixffn(y_gm, y_s5, y_fox, w_out, x, mod, g1_row, b1_row,
            w_up, conv_w, conv_b_row, w_down, g2_row, b2_row, layer, alpha):
    b, s, d = x.shape
    gm, s5w, fw = y_gm.shape[-1], y_s5.shape[-1], y_fox.shape[-1]
    dff = w_down.shape[1]
    tm = FFN_ROW_TILE
    chunk = 256
    row = lambda n: pl.BlockSpec((1, tm, n), lambda i, j: (i, j, 0))
    full = lambda a: _layer_spec(a, layer)
    resident = lambda a: pl.BlockSpec((None,) + a.shape[1:], lambda i, j: (layer,) + (0,) * (a.ndim - 1),
                                      pipeline_mode=pl.Buffered(1))
    return pl.pallas_call(
        functools.partial(_mixffn_kernel, alpha=alpha, gm=gm, s5w=s5w, dff=dff, chunk=chunk),
        grid=(b, s // tm),
        in_specs=[row(gm), row(s5w), row(fw), resident(w_out), row(d),
                  _mod_spec(mod, layer), full(g1_row), full(b1_row),
                  resident(w_up), full(conv_w), full(conv_b_row), resident(w_down),
                  full(g2_row), full(b2_row)],
        out_specs=row(d),
        out_shape=jax.ShapeDtypeStruct((b, s, d), F32),
        scratch_shapes=[pltpu.VMEM((tm, d), F32),
                        pltpu.VMEM((tm, d), BF16),
                        pltpu.VMEM((SUBLANES, dff), F32),
                        pltpu.VMEM((tm, dff), BF16)],
        compiler_params=_cparams(2),
        name="mixffn",
    )(y_gm, y_s5, y_fox, w_out, x, mod, g1_row, b1_row,
      w_up, conv_w, conv_b_row, w_down, g2_row, b2_row)


def _block_diag(a):
    depth, g, r, c = a.shape
    eye = jnp.eye(g, dtype=a.dtype)
    return (a[:, :, :, None, :] * eye[None, :, None, :, None]).reshape(depth, g * r, g * c)


def kernel(x, c, w_ada, b_ada, w_in, b_f, gm_ln_g, gm_ln_b, gm_w_s, gm_b_s, s5_lam_re, s5_lam_im, s5_log_dt, s5_b_re, s5_b_im, s5_c_re, s5_c_im, s5_d, s5_w_glu, s5_b_glu, w_out, ln1_g, ln1_b, w_up, conv_w, conv_b, w_down, ln2_g, ln2_b):
    depth, d, _ = w_in.shape
    batch = x.shape[0]
    gm_heads = gm_w_s.shape[1]
    gm = gm_heads * HEAD_DIM
    groups = s5_lam_re.shape[1]
    s5w = groups * S5_GROUP_DIM
    fox_heads = b_f.shape[1]
    fw = fox_heads * HEAD_DIM
    n_main = 2 * gm + s5w + 3 * fw
    alpha = (2.0 * depth) ** 0.25

    pad_rows = 2 * SUBLANES
    c_pad = jnp.pad(c, ((0, pad_rows - batch), (0, 0)))
    mod = _adaln(c_pad, w_ada, b_ada[:, None, :])[:, :batch]
    mod = mod.reshape(depth, batch, 6, d)

    n_state = groups * S5_STATE
    lre = s5_lam_re.reshape(depth, 1, n_state)
    lim = s5_lam_im.reshape(depth, 1, n_state)
    ldt = jnp.repeat(s5_log_dt, S5_STATE, axis=-1).reshape(depth, 1, n_state)
    bre_blk = _block_diag(jnp.swapaxes(s5_b_re, -1, -2))
    bim_blk = _block_diag(jnp.swapaxes(s5_b_im, -1, -2))
    cre_blk = _block_diag(jnp.swapaxes(s5_c_re, -1, -2))
    cim_blk = _block_diag(jnp.swapaxes(s5_c_im, -1, -2))
    lam, laml, bblk, cblk = _s5prep(lre, lim, ldt, bre_blk, bim_blk, cre_blk, cim_blk)
    wglu_blk = _block_diag(jnp.swapaxes(s5_w_glu, -1, -2)).astype(BF16)

    w_in_bf = jnp.pad(w_in, ((0, 0), (0, 0), (0, LANES - fox_heads))).astype(BF16)
    w_out_bf = w_out.astype(BF16)
    w_up_bf = w_up.astype(BF16)
    w_down_bf = w_down.astype(BF16)
    row3 = lambda a: a.reshape(depth, 1, -1)
    gm_g, gm_b = row3(gm_ln_g), row3(gm_ln_b)
    bs_full = jnp.repeat(jnp.swapaxes(gm_b_s, 1, 2), HEAD_DIM, axis=2)
    s5_d_row, s5_bglu_row = row3(s5_d), row3(s5_b_glu)
    bf_row = jnp.pad(b_f[:, None, :], ((0, 0), (0, 0), (0, LANES - fox_heads)))
    ln1_g3, ln1_b3, ln2_g3, ln2_b3 = row3(ln1_g), row3(ln1_b), row3(ln2_g), row3(ln2_b)
    conv_b3 = row3(conv_b)

    for l in range(depth):
        y_gm, s5_a, s5_b, v, q_aug, k_aug = _premix(x, mod, w_in_bf, gm_g, gm_b, gm_w_s, bs_full,
                                                    bf_row, l, gm, s5w, fw)
        y_s5 = _s5(s5_a, s5_b, lam[l], laml[l], bblk[l], cblk[l], s5_d_row[l], wglu_blk[l],
                   s5_bglu_row[l])
        y_fox = _fox(q_aug, k_aug, v)
        x = _mixffn(y_gm, y_s5, y_fox, w_out_bf, x, mod, ln1_g3, ln1_b3,
                    w_up_bf, conv_w, conv_b3, w_down_bf, ln2_g3, ln2_b3, l, alpha)
    return x
```

```python
import functools
import math

import jax
import jax.numpy as jnp
from jax import lax
from jax.experimental import pallas as pl
from jax.experimental.pallas import tpu as pltpu

F32 = jnp.float32
BF16 = jnp.bfloat16

HEAD_DIM = 64
GM_CHUNK = 128
S5_GROUP_DIM = 16
S5_STATE = 64
CONV_WIDTH = 3
LN_EPS = 1e-5
NEG_INF = -1e30
LOG2E = math.log2(math.e)

LANES = 128
SUBLANES = 8
ROW_TILE = 512
FFN_ROW_TILE = 512
S5_CHUNK = 128
S5_NCHUNK = SUBLANES
FOX_BLOCK = 512
FOX_HEADS_PER_STEP = 2
VMEM_LIMIT = 56 * 1024 * 1024


def _cparams(n_axes):
    return pltpu.CompilerParams(dimension_semantics=("arbitrary",) * n_axes,
                                vmem_limit_bytes=VMEM_LIMIT)


def _layer_norm_rows(y, g, b):
    mu = jnp.mean(y, axis=-1, keepdims=True)
    d = y - mu
    var = jnp.mean(d * d, axis=-1, keepdims=True)
    return d * lax.rsqrt(var + LN_EPS) * g + b


def _gelu_tanh(x):
    c = math.sqrt(2.0 / math.pi)
    return 0.5 * x * (1.0 + jnp.tanh(c * (x + 0.044715 * (x * x * x))))


def _split3(x):
    hi = x.astype(BF16)
    r1 = x - hi.astype(F32)
    mid = r1.astype(BF16)
    lo = (r1 - mid.astype(F32)).astype(BF16)
    return hi, mid, lo


def _adaln_kernel(c_ref, w_ref, b_ref, o_ref):
    c = c_ref[...]
    cond = (c * jax.nn.sigmoid(c)).astype(BF16)
    w = w_ref[0].astype(BF16)
    o_ref[0] = jnp.dot(cond, w, preferred_element_type=F32) + b_ref[0]


def _adaln(c_pad, w_ada, b_ada3):
    depth, d, n = w_ada.shape
    rows = c_pad.shape[0]
    tn = 1536
    return pl.pallas_call(
        _adaln_kernel,
        grid=(depth, n // tn),
        in_specs=[pl.BlockSpec((rows, d), lambda l, j: (0, 0)),
                  pl.BlockSpec((1, d, tn), lambda l, j: (l, 0, j)),
                  pl.BlockSpec((1, 1, tn), lambda l, j: (l, 0, j))],
        out_specs=pl.BlockSpec((1, rows, tn), lambda l, j: (l, 0, j)),
        out_shape=jax.ShapeDtypeStruct((depth, rows, n), F32),
        compiler_params=_cparams(2),
        name="adaln",
    )(c_pad, w_ada, b_ada3)


def _layer_spec(a, layer, block=None):
    shape = tuple(a.shape[1:]) if block is None else tuple(block)
    return pl.BlockSpec((None,) + shape, lambda *_: (layer,) + (0,) * len(shape))


def _mod_spec(mod, layer):
    return pl.BlockSpec((None, 1) + tuple(mod.shape[2:]), lambda i, *_: (layer, i, 0, 0))


def _gmlp_tile(u_ref, v_ref, g_ref, b_ref, ws_ref, bs_ref, o_ref, heads, row0):
    v = v_ref[...]
    rows, width = v.shape
    lane = lax.broadcasted_iota(jnp.int32, (rows, width), 1)
    head_masks = [(lane >= h * HEAD_DIM) & (lane < (h + 1) * HEAD_DIM) for h in range(heads)]

    def seg_mean(a):
        out = jnp.zeros_like(a)
        for m in head_masks:
            s = jnp.sum(jnp.where(m, a, 0.0), axis=-1, keepdims=True) * (1.0 / HEAD_DIM)
            out = jnp.where(m, s, out)
        return out

    d = v - seg_mean(v)
    var = seg_mean(d * d)
    vn = (d * lax.rsqrt(var + LN_EPS) * g_ref[...] + b_ref[...]).astype(BF16)

    r = lax.broadcasted_iota(jnp.int32, (GM_CHUNK, GM_CHUNK), 0)
    c = lax.broadcasted_iota(jnp.int32, (GM_CHUNK, GM_CHUNK), 1)
    tril = c <= r
    w_cat = jnp.concatenate([jnp.where(tril, ws_ref[h], 0.0).astype(BF16) for h in range(heads)],
                            axis=1)
    lane_c = lax.broadcasted_iota(jnp.int32, (GM_CHUNK, width), 1)
    chunk_masks = [(lane_c >= h * HEAD_DIM) & (lane_c < (h + 1) * HEAD_DIM) for h in range(heads)]
    bs = bs_ref[...]
    zero = jnp.zeros((), BF16)
    for j in range(rows // GM_CHUNK):
        r0 = j * GM_CHUNK
        vc = vn[r0:r0 + GM_CHUNK]
        stacked = jnp.concatenate([jnp.where(chunk_masks[h], vc, zero) for h in range(heads)],
                                  axis=0)
        z = jnp.dot(w_cat, stacked, preferred_element_type=F32)
        o_ref[0, row0 + r0:row0 + r0 + GM_CHUNK, :] = (
            u_ref[r0:r0 + GM_CHUNK, :] * (z + bs)).astype(BF16)


def _s5prep_kernel(lre_ref, lim_ref, ldt_ref, bre_ref, bim_ref, cre_ref, cim_ref,
                   lam_ref, laml_ref, bblk_ref, cblk_ref):
    lre = lre_ref[0]
    lim = lim_ref[0]
    dt = jnp.exp(ldt_ref[0])
    mag = jnp.exp(lre * dt)
    lbr = mag * jnp.cos(lim * dt)
    lbi = mag * jnp.sin(lim * dt)
    nr = lbr - 1.0
    den = lre * lre + lim * lim
    cr = (nr * lre + lbi * lim) / den
    ci = (lbi * lre - nr * lim) / den
    n = lre.shape[-1]
    lam_ref[0, :, 0:n] = lbr
    lam_ref[0, :, n:2 * n] = lbi
    pr, pi = lbr, lbi
    for _ in range(int(math.log2(S5_CHUNK))):
        pr, pi = pr * pr - pi * pi, 2.0 * (pr * pi)
    laml_ref[0, :, 0:n] = pr
    laml_ref[0, :, n:2 * n] = pi
    bre = bre_ref[0]
    bim = bim_ref[0]
    bblk_ref[0, :, 0:n] = (cr * bre - ci * bim).astype(BF16)
    bblk_ref[0, :, n:2 * n] = (cr * bim + ci * bre).astype(BF16)
    cblk_ref[0, 0:n, :] = cre_ref[0].astype(BF16)
    cblk_ref[0, n:2 * n, :] = (-cim_ref[0]).astype(BF16)


def _s5prep(lre, lim, ldt, bre_blk, bim_blk, cre_blk, cim_blk):
    depth, _, n = lre.shape
    w = bre_blk.shape[1]
    vec = pl.BlockSpec((1, 1, n), lambda l: (l, 0, 0))
    return pl.pallas_call(
        _s5prep_kernel,
        grid=(depth,),
        in_specs=[vec, vec, vec,
                  pl.BlockSpec((1, w, n), lambda l: (l, 0, 0)),
                  pl.BlockSpec((1, w, n), lambda l: (l, 0, 0)),
                  pl.BlockSpec((1, n, w), lambda l: (l, 0, 0)),
                  pl.BlockSpec((1, n, w), lambda l: (l, 0, 0))],
        out_specs=[pl.BlockSpec((1, 1, 2 * n), lambda l: (l, 0, 0)),
                   pl.BlockSpec((1, 1, 2 * n), lambda l: (l, 0, 0)),
                   pl.BlockSpec((1, w, 2 * n), lambda l: (l, 0, 0)),
                   pl.BlockSpec((1, 2 * n, w), lambda l: (l, 0, 0))],
        out_shape=[jax.ShapeDtypeStruct((depth, 1, 2 * n), F32),
                   jax.ShapeDtypeStruct((depth, 1, 2 * n), F32),
                   jax.ShapeDtypeStruct((depth, w, 2 * n), BF16),
                   jax.ShapeDtypeStruct((depth, 2 * n, w), BF16)],
        compiler_params=_cparams(1),
        name="s5prep",
    )(lre, lim, ldt, bre_blk, bim_blk, cre_blk, cim_blk)


def _s5_kernel(ua_ref, ub_ref, lam_ref, laml_ref, bblk_ref, cblk_ref, d_ref, wglu_ref, bglu_ref,
               o_ref, r_scr, xs_scr, e_scr, xin_scr, carry, y_scr, *, n):
    nc, lc = S5_NCHUNK, S5_CHUNK
    m_rows = nc * lc
    pair = ua_ref.shape[0]

    @pl.when(pl.program_id(1) == 0)
    def _():
        carry[...] = jnp.zeros_like(carry)

    def u_rows(s, r0, r1):
        return jnp.concatenate([ua_ref[s, r0:r1, :], ub_ref[s, r0:r1, :]], axis=1)

    lr = jnp.broadcast_to(lam_ref[:, 0:n], (nc, n))
    li = jnp.broadcast_to(lam_ref[:, n:2 * n], (nc, n))

    def expand(s):
        r_scr[s] = jnp.dot(u_rows(s, 0, m_rows).astype(BF16), bblk_ref[...],
                           preferred_element_type=F32)

    def scan(s, xr, xi, store):
        for t2 in range(lc // 2):
            r0 = t2 * 2 * nc
            kept = []
            for half in range(2):
                br = r_scr[s, r0 + half * nc:r0 + (half + 1) * nc, 0:n]
                bi = r_scr[s, r0 + half * nc:r0 + (half + 1) * nc, n:2 * n]
                xr, xi = lr * xr - li * xi + br, lr * xi + li * xr + bi
                kept.append((xr, xi))
            if store:
                xs_scr[s, r0:r0 + 2 * nc, 0:n] = jnp.concatenate(
                    [kept[0][0], kept[1][0]], axis=0).astype(BF16)
                xs_scr[s, r0:r0 + 2 * nc, n:2 * n] = jnp.concatenate(
                    [kept[0][1], kept[1][1]], axis=0).astype(BF16)
        return xr, xi

    def scans(s):
        zero = jnp.zeros((nc, n), F32)
        er, ei = scan(s, zero, zero, False)
        e_scr[s, :, 0:n] = er
        e_scr[s, :, n:2 * n] = ei
        llr = laml_ref[:, 0:n]
        lli = laml_ref[:, n:2 * n]
        cur_r = carry[s, :, 0:n]
        cur_i = carry[s, :, n:2 * n]
        for c in range(nc):
            xin_scr[s, c:c + 1, 0:n] = cur_r
            xin_scr[s, c:c + 1, n:2 * n] = cur_i
            e_r = e_scr[s, c:c + 1, 0:n]
            e_i = e_scr[s, c:c + 1, n:2 * n]
            cur_r, cur_i = llr * cur_r - lli * cur_i + e_r, llr * cur_i + lli * cur_r + e_i
        carry[s, :, 0:n] = cur_r
        carry[s, :, n:2 * n] = cur_i
        scan(s, xin_scr[s, :, 0:n], xin_scr[s, :, n:2 * n], True)

    def readout(s):
        step = 256
        for r0 in range(0, m_rows, step):
            y = jnp.dot(xs_scr[s, r0:r0 + step, :], cblk_ref[...], preferred_element_type=F32)
            y = y + d_ref[...] * u_rows(s, r0, r0 + step)
            y = _gelu_tanh(y)
            gate = (jnp.dot(y.astype(BF16), wglu_ref[...], preferred_element_type=F32)
                    + bglu_ref[...])
            out = y * jax.nn.sigmoid(gate)
            for hf in range(out.shape[1] // LANES):
                y_scr[s, hf, r0:r0 + step, :] = out[:, hf * LANES:(hf + 1) * LANES]
        for c in range(nc):
            for hf in range(y_scr.shape[1]):
                o_ref[s, c * lc:(c + 1) * lc, hf * LANES:(hf + 1) * LANES] = (
                    y_scr[s, hf, pl.ds(c, lc, stride=nc), :].astype(BF16))

    for s in range(pair):
        expand(s)
    for s in range(pair):
        scans(s)
        readout(s)


def _s5(u_a, u_b, lam, laml, bblk, cblk, d_row, wglu_blk, bglu_row):
    b, s, _ = u_a.shape
    w = 2 * LANES
    n = lam.shape[-1] // 2
    m_rows = S5_NCHUNK * S5_CHUNK
    pair = 2
    full = lambda a: pl.BlockSpec(a.shape, lambda i, j: (0,) * a.ndim)
    half = pl.BlockSpec((pair, m_rows, LANES), lambda i, j: (i, j, 0))
    return pl.pallas_call(
        functools.partial(_s5_kernel, n=n),
        grid=(b // pair, s // m_rows),
        in_specs=[half, half, full(lam), full(laml), full(bblk), full(cblk),
                  full(d_row), full(wglu_blk), full(bglu_row)],
        out_specs=pl.BlockSpec((pair, m_rows, w), lambda i, j: (i, j, 0)),
        out_shape=jax.ShapeDtypeStruct((b, s, w), BF16),
        scratch_shapes=[pltpu.VMEM((pair, m_rows, 2 * n), F32),
                        pltpu.VMEM((pair, m_rows, 2 * n), BF16),
                        pltpu.VMEM((pair, S5_NCHUNK, 2 * n), F32),
                        pltpu.VMEM((pair, S5_NCHUNK, 2 * n), F32),
                        pltpu.VMEM((pair, 1, 2 * n), F32),
                        pltpu.VMEM((pair, w // LANES, m_rows, LANES), F32)],
        compiler_params=_cparams(2),
        name="s5",
    )(u_a, u_b, lam, laml, bblk, cblk, d_row, wglu_blk, bglu_row)


def _foxprep_tile(fl, q_ref, k_ref, bf_ref, qa_ref, ka_ref, carry, cum_scr, heads, row0):
    rows = fl.shape[0]
    x = fl + bf_ref[...]
    lane = lax.broadcasted_iota(jnp.int32, (rows, LANES), 1)
    ls = jnp.minimum(x, 0.0) - jnp.log1p(jnp.exp(-jnp.abs(x)))
    ls = jnp.where(lane < heads, ls, 0.0)

    r = lax.broadcasted_iota(jnp.int32, (LANES, LANES), 0)
    c = lax.broadcasted_iota(jnp.int32, (LANES, LANES), 1)
    ltri = jnp.where(c <= r, 1.0, 0.0).astype(BF16)
    run = carry[...]
    for j in range(rows // LANES):
        blk = ls[j * LANES:(j + 1) * LANES]
        parts = jnp.dot(ltri, jnp.concatenate(_split3(blk), axis=1), preferred_element_type=F32)
        loc = parts[:, 0:LANES] + parts[:, LANES:2 * LANES] + parts[:, 2 * LANES:3 * LANES]
        cum_blk = loc + run
        run = cum_blk[LANES - 1:LANES, :]
        cum_scr[j * LANES:(j + 1) * LANES, :] = cum_blk
    carry[...] = run

    chi, cmid, clo = _split3(cum_scr[...] * LOG2E)
    packed = (chi.astype(F32) + pltpu.roll(cmid.astype(F32), heads, 1)
              + pltpu.roll(clo.astype(F32), 2 * heads, 1))
    lane1 = lax.broadcasted_iota(jnp.int32, (1, LANES), 1)
    for h in range(heads):
        pair = (h // 2) * LANES
        keep = (lane < HEAD_DIM) if h % 2 == 0 else (lane >= HEAD_DIM)
        base = HEAD_DIM if h % 2 == 0 else 0
        lanes_of = lambda first: sum(
            jnp.where(lane1 == base + (first + p) * heads + h, 1.0, 0.0) for p in range(3))
        part_lanes, ones_lanes = lanes_of(0), lanes_of(3)
        at_parts = packed if base == 0 else pltpu.roll(packed, base, 1)
        at_ones = pltpu.roll(packed, base + 3 * heads, 1)
        q_bias = (at_parts * part_lanes + ones_lanes).astype(BF16)
        k_bias = (part_lanes - at_ones * ones_lanes).astype(BF16)
        qa_ref[0, h, row0:row0 + rows, :] = jnp.where(keep, q_ref[:, pair:pair + LANES], q_bias)
        ka_ref[0, h, row0:row0 + rows, :] = jnp.where(keep, k_ref[:, pair:pair + LANES], k_bias)


def _premix_kernel(x_ref, mod_ref, w_ref, wf_ref, lng_ref, lnb_ref, ws_ref, bs_ref, bf_ref,
                   ygm_ref, s5a_ref, s5b_ref, v_ref, qa_ref, ka_ref,
                   h_scr, gu_scr, gv_scr, q_scr, k_scr, carry, cum_scr,
                   *, gm, s5w, fw, gm_heads, fox_heads):
    @pl.when(pl.program_id(1) == 0)
    def _():
        carry[...] = jnp.zeros_like(carry)

    n_sub, tm = h_scr.shape[0], h_scr.shape[1]
    assert n_sub * tm == S5_NCHUNK * S5_CHUNK
    sh = mod_ref[0, 0:1, :]
    sc = mod_ref[0, 1:2, :]
    fls = []
    for sub in range(n_sub):
        row0 = sub * tm
        h_scr[sub] = (x_ref[0, row0:row0 + tm, :] * (1.0 + sc) + sh).astype(BF16)
        h = h_scr[sub]

        def proj(lo, hi):
            return jnp.dot(h, w_ref[:, lo:hi], preferred_element_type=F32)

        o = 0
        gu_scr[sub] = proj(o, o + gm); o += gm
        gv_scr[sub] = proj(o, o + gm); o += gm
        u = proj(o, o + s5w); o += s5w
        chunk0 = row0 // S5_CHUNK
        for cl in range(tm // S5_CHUNK):
            for hf, ref in enumerate((s5a_ref, s5b_ref)):
                ref[0, pl.ds(chunk0 + cl, S5_CHUNK, stride=S5_NCHUNK), :] = (
                    u[cl * S5_CHUNK:(cl + 1) * S5_CHUNK, hf * LANES:(hf + 1) * LANES])
        q_scr[sub] = (proj(o, o + fw) * (LOG2E * HEAD_DIM ** -0.5)).astype(BF16); o += fw
        k_scr[sub] = proj(o, o + fw).astype(BF16); o += fw
        v_ref[0, row0:row0 + tm, :] = proj(o, o + fw).astype(BF16)
        fls.append(jnp.dot(h, wf_ref[...], preferred_element_type=F32))

    for sub in range(n_sub):
        row0 = sub * tm
        _gmlp_tile(gu_scr.at[sub], gv_scr.at[sub], lng_ref, lnb_ref, ws_ref, bs_ref, ygm_ref,
                   gm_heads, row0)
        _foxprep_tile(fls[sub], q_scr.at[sub], k_scr.at[sub], bf_ref, qa_ref, ka_ref, carry,
                      cum_scr.at[sub], fox_heads, row0)


def _premix(x, mod, w_in, ln_g_row, ln_b_row, w_s, b_s_full, bf_row, layer, gm, s5w, fw):
    b, s, d = x.shape
    tm = ROW_TILE
    blk = S5_NCHUNK * S5_CHUNK
    n_sub = blk // tm
    n_main = 2 * gm + s5w + 3 * fw
    w_main_spec = _layer_spec(w_in, layer, (d, n_main))
    w_f_spec = pl.BlockSpec((None, d, LANES), lambda *_: (layer, 0, n_main // LANES))
    gm_heads = w_s.shape[1]
    fox_heads = fw // HEAD_DIM
    row = lambda n: pl.BlockSpec((1, blk, n), lambda i, j: (i, j, 0))
    full = lambda a: pl.BlockSpec(a.shape, lambda i, j: (0,) * a.ndim)
    aug = pl.BlockSpec((1, fox_heads, blk, LANES), lambda i, j: (i, 0, j, 0))
    lay = lambda a: _layer_spec(a, layer)
    assert s5w == 2 * LANES
    s5_half = row(LANES)
    outs = [jax.ShapeDtypeStruct((b, s, gm), BF16),
            jax.ShapeDtypeStruct((b, s, LANES), F32), jax.ShapeDtypeStruct((b, s, LANES), F32),
            jax.ShapeDtypeStruct((b, s, fw), BF16),
            jax.ShapeDtypeStruct((b, fox_heads, s, LANES), BF16),
            jax.ShapeDtypeStruct((b, fox_heads, s, LANES), BF16)]
    return pl.pallas_call(
        functools.partial(_premix_kernel, gm=gm, s5w=s5w, fw=fw, gm_heads=gm_heads,
                          fox_heads=fox_heads),
        grid=(b, s // blk),
        in_specs=[row(d), _mod_spec(mod, layer), w_main_spec, w_f_spec, lay(ln_g_row),
                  lay(ln_b_row), lay(w_s), lay(b_s_full), lay(bf_row)],
        out_specs=[row(gm), s5_half, s5_half, row(fw), aug, aug],
        out_shape=outs,
        scratch_shapes=[pltpu.VMEM((n_sub, tm, d), BF16),
                        pltpu.VMEM((n_sub, tm, gm), F32), pltpu.VMEM((n_sub, tm, gm), F32),
                        pltpu.VMEM((n_sub, tm, fw), BF16), pltpu.VMEM((n_sub, tm, fw), BF16),
                        pltpu.VMEM((1, LANES), F32), pltpu.VMEM((n_sub, tm, LANES), F32)],
        compiler_params=_cparams(2),
        name="premix",
    )(x, mod, w_in, w_in, ln_g_row, ln_b_row, w_s, b_s_full, bf_row)


def _fox_kernel(qa_ref, ka_ref, v_ref, o_ref, va_scr):
    nh = qa_ref.shape[1]
    s_len = qa_ref.shape[2]
    tq = FOX_BLOCK
    tk = tq
    nt = (((1,), (1,)), ((), ()))

    lane_v = lax.broadcasted_iota(jnp.int32, va_scr.shape[1:], 1)
    one = jnp.ones((), BF16)
    for hp in range(nh // 2):
        v2 = v_ref[0, :, hp * LANES:(hp + 1) * LANES]
        va_scr[2 * hp] = jnp.where(lane_v < HEAD_DIM, v2, one)
        va_scr[2 * hp + 1] = jnp.where(lane_v >= HEAD_DIM, v2, one)

    def step(q0, k0, width, state, mask_off):
        logits = [lax.dot_general(qa_ref[0, hh, q0:q0 + tq, :], ka_ref[0, hh, k0:k0 + width, :],
                                  nt, preferred_element_type=F32) for hh in range(nh)]
        new = []
        for hh in range(nh):
            m, acc = state[hh]
            s = logits[hh]
            if mask_off is not None:
                row = lax.broadcasted_iota(jnp.int32, (tq, width), 0)
                col = lax.broadcasted_iota(jnp.int32, (tq, width), 1)
                s = jnp.where(col <= row + mask_off, s, NEG_INF)
            m_new = jnp.maximum(m, jnp.max(s, axis=-1, keepdims=True))
            alpha = jnp.exp2(m - m_new)
            p = jnp.exp2(s - m_new)
            pv = jnp.dot(p.astype(BF16), va_scr[hh, k0:k0 + width, :],
                         preferred_element_type=F32)
            new.append((m_new, alpha * acc + pv))
        return tuple(new)

    wide = 2 * tk
    lane = lax.broadcasted_iota(jnp.int32, (tq, LANES), 1)
    for qi in range(s_len // tq):
        q0 = qi * tq
        state = tuple((jnp.full((tq, 1), -jnp.inf, F32), jnp.zeros((tq, LANES), F32))
                      for _ in range(nh))
        for j in range(qi // 2):
            state = step(q0, j * wide, wide, state, None)
        k_tail = (qi // 2) * wide
        if qi % 2 == 1:
            state = step(q0, k_tail, wide, state, tk)
        else:
            state = step(q0, k_tail, tk, state, 0)
        outs = [acc / pltpu.roll(acc, HEAD_DIM, 1) for _, acc in state]
        for hp in range(nh // 2):
            o_ref[0, q0:q0 + tq, hp * LANES:(hp + 1) * LANES] = jnp.where(
                lane < HEAD_DIM, outs[2 * hp], outs[2 * hp + 1]).astype(BF16)


def _fox(q_aug, k_aug, v):
    b, heads, s, _ = q_aug.shape
    fw = v.shape[-1]
    nh = FOX_HEADS_PER_STEP
    vw = nh // 2 * LANES
    seq = lambda: pl.BlockSpec((1, nh, s, LANES), lambda i, p: (i, p, 0, 0))
    return pl.pallas_call(
        _fox_kernel,
        grid=(b, heads // nh),
        in_specs=[seq(), seq(), pl.BlockSpec((1, s, vw), lambda i, p: (i, 0, p))],
        out_specs=pl.BlockSpec((1, s, vw), lambda i, p: (i, 0, p)),
        out_shape=jax.ShapeDtypeStruct((b, s, fw), BF16),
        scratch_shapes=[pltpu.VMEM((nh, s, LANES), BF16)],
        compiler_params=_cparams(2),
        name="fox",
    )(q_aug, k_aug, v)


def _mixffn_kernel(ygm_ref, ys5_ref, yfox_ref, wout_ref, x_ref, mod_ref, g1_ref, b1_ref,
                   wup_ref, cw_ref, cb_ref, wdn_ref, g2_ref, b2_ref, o_ref,
                   x1_scr, h_scr, halo, act_scr, *, alpha, gm, s5w, dff, chunk):
    @pl.when(pl.program_id(1) == 0)
    def _():
        halo[...] = jnp.zeros_like(halo)

    mix = jnp.dot(ygm_ref[0], wout_ref[0:gm, :], preferred_element_type=F32)
    mix += jnp.dot(ys5_ref[0], wout_ref[gm:gm + s5w, :], preferred_element_type=F32)
    mix += jnp.dot(yfox_ref[0], wout_ref[gm + s5w:, :], preferred_element_type=F32)
    gate1 = mod_ref[0, 2:3, :]
    x1_scr[...] = _layer_norm_rows(alpha * x_ref[0] + (1.0 + gate1) * mix, g1_ref[...], b1_ref[...])

    rows = x_ref.shape[1]
    sh = mod_ref[0, 3:4, :]
    sc = mod_ref[0, 4:5, :]
    gate = mod_ref[0, 5:6, :]
    h_scr[...] = (x1_scr[...] * (1.0 + sc) + sh).astype(BF16)
    h = h_scr[...]
    rid = lax.broadcasted_iota(jnp.int32, (rows, chunk), 0)
    for ci in range(dff // chunk):
        c0 = ci * chunk
        a = jnp.dot(h, wup_ref[:, c0:c0 + chunk], preferred_element_type=F32)
        gt = jnp.dot(h, wup_ref[:, dff + c0:dff + c0 + chunk], preferred_element_type=F32)
        p1 = halo[SUBLANES - 1:SUBLANES, c0:c0 + chunk]
        p2 = halo[SUBLANES - 2:SUBLANES - 1, c0:c0 + chunk]
        a1 = jnp.where(rid == 0, p1, pltpu.roll(a, 1, 0))
        a2 = jnp.where(rid == 0, p2, jnp.where(rid == 1, p1, pltpu.roll(a, 2, 0)))
        halo[:, c0:c0 + chunk] = a[rows - SUBLANES:rows, :]
        conv = (cb_ref[:, c0:c0 + chunk] + cw_ref[0:1, c0:c0 + chunk] * a2
                + cw_ref[1:2, c0:c0 + chunk] * a1 + cw_ref[2:3, c0:c0 + chunk] * a)
        act_scr[:, c0:c0 + chunk] = (_gelu_tanh(conv) * gt).astype(BF16)
    ffn = jnp.dot(act_scr[...], wdn_ref[...], preferred_element_type=F32)
    y = alpha * x1_scr[...] + (1.0 + gate) * ffn
    o_ref[0] = _layer_norm_rows(y, g2_ref[...], b2_ref[...])


def _mixffn(y_gm, y_s5, y_fox, w_out, x, mod, g1_row, b1_row,
            w_up, conv_w, conv_b_row, w_down, g2_row, b2_row, layer, alpha):
    b, s, d = x.shape
    gm, s5w, fw = y_gm.shape[-1], y_s5.shape[-1], y_fox.shape[-1]
    dff = w_down.shape[1]
    tm = FFN_ROW_TILE
    chunk = 256
    row = lambda n: pl.BlockSpec((1, tm, n), lambda i, j: (i, j, 0))
    full = lambda a: _layer_spec(a, layer)
    resident = lambda a: pl.BlockSpec((None,) + a.shape[1:], lambda i, j: (layer,) + (0,) * (a.ndim - 1),
                                      pipeline_mode=pl.Buffered(1))
    return pl.pallas_call(
        functools.partial(_mixffn_kernel, alpha=alpha, gm=gm, s5w=s5w, dff=dff, chunk=chunk),
        grid=(b, s // tm),
        in_specs=[row(gm), row(s5w), row(fw), resident(w_out), row(d),
                  _mod_spec(mod, layer), full(g1_row), full(b1_row),
                  resident(w_up), full(conv_w), full(conv_b_row), resident(w_down),
                  full(g2_row), full(b2_row)],
        out_specs=row(d),
        out_shape=jax.ShapeDtypeStruct((b, s, d), F32),
        scratch_shapes=[pltpu.VMEM((tm, d), F32),
                        pltpu.VMEM((tm, d), BF16),
                        pltpu.VMEM((SUBLANES, dff), F32),
                        pltpu.VMEM((tm, dff), BF16)],
        compiler_params=_cparams(2),
        name="mixffn",
    )(y_gm, y_s5, y_fox, w_out, x, mod, g1_row, b1_row,
      w_up, conv_w, conv_b_row, w_down, g2_row, b2_row)


def _block_diag(a):
    depth, g, r, c = a.shape
    eye = jnp.eye(g, dtype=a.dtype)
    return (a[:, :, :, None, :] * eye[None, :, None, :, None]).reshape(depth, g * r, g * c)


def kernel(x, c, w_ada, b_ada, w_in, b_f, gm_ln_g, gm_ln_b, gm_w_s, gm_b_s, s5_lam_re, s5_lam_im, s5_log_dt, s5_b_re, s5_b_im, s5_c_re, s5_c_im, s5_d, s5_w_glu, s5_b_glu, w_out, ln1_g, ln1_b, w_up, conv_w, conv_b, w_down, ln2_g, ln2_b):
    depth, d, _ = w_in.shape
    batch = x.shape[0]
    gm_heads = gm_w_s.shape[1]
    gm = gm_heads * HEAD_DIM
    groups = s5_lam_re.shape[1]
    s5w = groups * S5_GROUP_DIM
    fox_heads = b_f.shape[1]
    fw = fox_heads * HEAD_DIM
    n_main = 2 * gm + s5w + 3 * fw
    alpha = (2.0 * depth) ** 0.25

    pad_rows = 2 * SUBLANES
    c_pad = jnp.pad(c, ((0, pad_rows - batch), (0, 0)))
    mod = _adaln(c_pad, w_ada, b_ada[:, None, :])[:, :batch]
    mod = mod.reshape(depth, batch, 6, d)

    n_state = groups * S5_STATE
    lre = s5_lam_re.reshape(depth, 1, n_state)
    lim = s5_lam_im.reshape(depth, 1, n_state)
    ldt = jnp.repeat(s5_log_dt, S5_STATE, axis=-1).reshape(depth, 1, n_state)
    bre_blk = _block_diag(jnp.swapaxes(s5_b_re, -1, -2))
    bim_blk = _block_diag(jnp.swapaxes(s5_b_im, -1, -2))
    cre_blk = _block_diag(jnp.swapaxes(s5_c_re, -1, -2))
    cim_blk = _block_diag(jnp.swapaxes(s5_c_im, -1, -2))
    lam, laml, bblk, cblk = _s5prep(lre, lim, ldt, bre_blk, bim_blk, cre_blk, cim_blk)
    wglu_blk = _block_diag(jnp.swapaxes(s5_w_glu, -1, -2)).astype(BF16)

    w_in_bf = jnp.pad(w_in, ((0, 0), (0, 0), (0, LANES - fox_heads))).astype(BF16)
    w_out_bf = w_out.astype(BF16)
    w_up_bf = w_up.astype(BF16)
    w_down_bf = w_down.astype(BF16)
    row3 = lambda a: a.reshape(depth, 1, -1)
    gm_g, gm_b = row3(gm_ln_g), row3(gm_ln_b)
    bs_full = jnp.repeat(jnp.swapaxes(gm_b_s, 1, 2), HEAD_DIM, axis=2)
    s5_d_row, s5_bglu_row = row3(s5_d), row3(s5_b_glu)
    bf_row = jnp.pad(b_f[:, None, :], ((0, 0), (0, 0), (0, LANES - fox_heads)))
    ln1_g3, ln1_b3, ln2_g3, ln2_b3 = row3(ln1_g), row3(ln1_b), row3(ln2_g), row3(ln2_b)
    conv_b3 = row3(conv_b)

    for l in range(depth):
        y_gm, s5_a, s5_b, v, q_aug, k_aug = _premix(x, mod, w_in_bf, gm_g, gm_b, gm_w_s, bs_full,
                                                    bf_row, l, gm, s5w, fw)
        y_s5 = _s5(s5_a, s5_b, lam[l], laml[l], bblk[l], cblk[l], s5_d_row[l], wglu_blk[l],
                   s5_bglu_row[l])
        y_fox = _fox(q_aug, k_aug, v)
        x = _mixffn(y_gm, y_s5, y_fox, w_out_bf, x, mod, ln1_g3, ln1_b3,
                    w_up_bf, conv_w, conv_b3, w_down_bf, ln2_g3, ln2_b3, l, alpha)
    return x
```

```python
import functools
import math

import jax
import jax.numpy as jnp
from jax import lax
from jax.experimental import pallas as pl
from jax.experimental.pallas import tpu as pltpu

F32 = jnp.float32
BF16 = jnp.bfloat16

HEAD_DIM = 64
GM_CHUNK = 128
S5_GROUP_DIM = 16
S5_STATE = 64
CONV_WIDTH = 3
LN_EPS = 1e-5
NEG_INF = -1e30
LOG2E = math.log2(math.e)

LANES = 128
SUBLANES = 8
ROW_TILE = 512
FFN_ROW_TILE = 512
S5_CHUNK = 128
S5_NCHUNK = SUBLANES
FOX_BLOCK = 512
FOX_HEADS_PER_STEP = 2
VMEM_LIMIT = 56 * 1024 * 1024


def _cparams(n_axes):
    return pltpu.CompilerParams(dimension_semantics=("arbitrary",) * n_axes,
                                vmem_limit_bytes=VMEM_LIMIT)


def _layer_norm_rows(y, g, b):
    mu = jnp.mean(y, axis=-1, keepdims=True)
    d = y - mu
    var = jnp.mean(d * d, axis=-1, keepdims=True)
    return d * lax.rsqrt(var + LN_EPS) * g + b


def _gelu_tanh(x):
    c = math.sqrt(2.0 / math.pi)
    return 0.5 * x * (1.0 + jnp.tanh(c * (x + 0.044715 * (x * x * x))))


def _split3(x):
    hi = x.astype(BF16)
    r1 = x - hi.astype(F32)
    mid = r1.astype(BF16)
    lo = (r1 - mid.astype(F32)).astype(BF16)
    return hi, mid, lo


def _adaln_kernel(c_ref, w_ref, b_ref, o_ref):
    c = c_ref[...]
    cond = (c * jax.nn.sigmoid(c)).astype(BF16)
    w = w_ref[0].astype(BF16)
    o_ref[0] = jnp.dot(cond, w, preferred_element_type=F32) + b_ref[0]


def _adaln(c_pad, w_ada, b_ada3):
    depth, d, n = w_ada.shape
    rows = c_pad.shape[0]
    tn = 1536
    return pl.pallas_call(
        _adaln_kernel,
        grid=(depth, n // tn),
        in_specs=[pl.BlockSpec((rows, d), lambda l, j: (0, 0)),
                  pl.BlockSpec((1, d, tn), lambda l, j: (l, 0, j)),
                  pl.BlockSpec((1, 1, tn), lambda l, j: (l, 0, j))],
        out_specs=pl.BlockSpec((1, rows, tn), lambda l, j: (l, 0, j)),
        out_shape=jax.ShapeDtypeStruct((depth, rows, n), F32),
        compiler_params=_cparams(2),
        name="adaln",
    )(c_pad, w_ada, b_ada3)


def _layer_spec(a, layer, block=None):
    shape = tuple(a.shape[1:]) if block is None else tuple(block)
    return pl.BlockSpec((None,) + shape, lambda *_: (layer,) + (0,) * len(shape))


def _mod_spec(mod, layer):
    return pl.BlockSpec((None, 1) + tuple(mod.shape[2:]), lambda i, *_: (layer, i, 0, 0))


def _gmlp_tile(u_ref, v_ref, g_ref, b_ref, ws_ref, bs_ref, o_ref, heads, row0):
    v = v_ref[...]
    rows, width = v.shape
    lane = lax.broadcasted_iota(jnp.int32, (rows, width), 1)
    head_masks = [(lane >= h * HEAD_DIM) & (lane < (h + 1) * HEAD_DIM) for h in range(heads)]

    def seg_mean(a):
        out = jnp.zeros_like(a)
        for m in head_masks:
            s = jnp.sum(jnp.where(m, a, 0.0), axis=-1, keepdims=True) * (1.0 / HEAD_DIM)
            out = jnp.where(m, s, out)
        return out

    d = v - seg_mean(v)
    var = seg_mean(d * d)
    vn = (d * lax.rsqrt(var + LN_EPS) * g_ref[...] + b_ref[...]).astype(BF16)

    r = lax.broadcasted_iota(jnp.int32, (GM_CHUNK, GM_CHUNK), 0)
    c = lax.broadcasted_iota(jnp.int32, (GM_CHUNK, GM_CHUNK), 1)
    tril = c <= r
    w_cat = jnp.concatenate([jnp.where(tril, ws_ref[h], 0.0).astype(BF16) for h in range(heads)],
                            axis=1)
    lane_c = lax.broadcasted_iota(jnp.int32, (GM_CHUNK, width), 1)
    chunk_masks = [(lane_c >= h * HEAD_DIM) & (lane_c < (h + 1) * HEAD_DIM) for h in range(heads)]
    bs = bs_ref[...]
    zero = jnp.zeros((), BF16)
    for j in range(rows // GM_CHUNK):
        r0 = j * GM_CHUNK
        vc = vn[r0:r0 + GM_CHUNK]
        stacked = jnp.concatenate([jnp.where(chunk_masks[h], vc, zero) for h in range(heads)],
                                  axis=0)
        z = jnp.dot(w_cat, stacked, preferred_element_type=F32)
        o_ref[0, row0 + r0:row0 + r0 + GM_CHUNK, :] = (
            u_ref[r0:r0 + GM_CHUNK, :] * (z + bs)).astype(BF16)


def _s5prep_kernel(lre_ref, lim_ref, ldt_ref, bre_ref, bim_ref, cre_ref, cim_ref,
                   lam_ref, laml_ref, bblk_ref, cblk_ref):
    lre = lre_ref[0]
    lim = lim_ref[0]
    dt = jnp.exp(ldt_ref[0])
    mag = jnp.exp(lre * dt)
    lbr = mag * jnp.cos(lim * dt)
    lbi = mag * jnp.sin(lim * dt)
    nr = lbr - 1.0
    den = lre * lre + lim * lim
    cr = (nr * lre + lbi * lim) / den
    ci = (lbi * lre - nr * lim) / den
    n = lre.shape[-1]
    lam_ref[0, :, 0:n] = lbr
    lam_ref[0, :, n:2 * n] = lbi
    pr, pi = lbr, lbi
    for _ in range(int(math.log2(S5_CHUNK))):
        pr, pi = pr * pr - pi * pi, 2.0 * (pr * pi)
    laml_ref[0, :, 0:n] = pr
    laml_ref[0, :, n:2 * n] = pi
    bre = bre_ref[0]
    bim = bim_ref[0]
    bblk_ref[0, :, 0:n] = (cr * bre - ci * bim).astype(BF16)
    bblk_ref[0, :, n:2 * n] = (cr * bim + ci * bre).astype(BF16)
    cblk_ref[0, 0:n, :] = cre_ref[0].astype(BF16)
    cblk_ref[0, n:2 * n, :] = (-cim_ref[0]).astype(BF16)


def _s5prep(lre, lim, ldt, bre_blk, bim_blk, cre_blk, cim_blk):
    depth, _, n = lre.shape
    w = bre_blk.shape[1]
    vec = pl.BlockSpec((1, 1, n), lambda l: (l, 0, 0))
    return pl.pallas_call(
        _s5prep_kernel,
        grid=(depth,),
        in_specs=[vec, vec, vec,
                  pl.BlockSpec((1, w, n), lambda l: (l, 0, 0)),
                  pl.BlockSpec((1, w, n), lambda l: (l, 0, 0)),
                  pl.BlockSpec((1, n, w), lambda l: (l, 0, 0)),
                  pl.BlockSpec((1, n, w), lambda l: (l, 0, 0))],
        out_specs=[pl.BlockSpec((1, 1, 2 * n), lambda l: (l, 0, 0)),
                   pl.BlockSpec((1, 1, 2 * n), lambda l: (l, 0, 0)),
                   pl.BlockSpec((1, w, 2 * n), lambda l: (l, 0, 0)),
                   pl.BlockSpec((1, 2 * n, w), lambda l: (l, 0, 0))],
        out_shape=[jax.ShapeDtypeStruct((depth, 1, 2 * n), F32),
                   jax.ShapeDtypeStruct((depth, 1, 2 * n), F32),
                   jax.ShapeDtypeStruct((depth, w, 2 * n), BF16),
                   jax.ShapeDtypeStruct((depth, 2 * n, w), BF16)],
        compiler_params=_cparams(1),
        name="s5prep",
    )(lre, lim, ldt, bre_blk, bim_blk, cre_blk, cim_blk)


def _s5_kernel(ua_ref, ub_ref, lam_ref, laml_ref, bblk_ref, cblk_ref, d_ref, wglu_ref, bglu_ref,
               o_ref, r_scr, xs_scr, e_scr, xin_scr, carry, y_scr, *, n):
    nc, lc = S5_NCHUNK, S5_CHUNK
    m_rows = nc * lc
    pair = ua_ref.shape[0]

    @pl.when(pl.program_id(1) == 0)
    def _():
        carry[...] = jnp.zeros_like(carry)

    def u_rows(s, r0, r1):
        return jnp.concatenate([ua_ref[s, r0:r1, :], ub_ref[s, r0:r1, :]], axis=1)

    lr = jnp.broadcast_to(lam_ref[:, 0:n], (nc, n))
    li = jnp.broadcast_to(lam_ref[:, n:2 * n], (nc, n))

    def expand(s):
        r_scr[s] = jnp.dot(u_rows(s, 0, m_rows).astype(BF16), bblk_ref[...],
                           preferred_element_type=F32)

    def scan(s, xr, xi, store):
        for t2 in range(lc // 2):
            r0 = t2 * 2 * nc
            kept = []
            for half in range(2):
                br = r_scr[s, r0 + half * nc:r0 + (half + 1) * nc, 0:n]
                bi = r_scr[s, r0 + half * nc:r0 + (half + 1) * nc, n:2 * n]
                xr, xi = lr * xr - li * xi + br, lr * xi + li * xr + bi
                kept.append((xr, xi))
            if store:
                xs_scr[s, r0:r0 + 2 * nc, 0:n] = jnp.concatenate(
                    [kept[0][0], kept[1][0]], axis=0).astype(BF16)
                xs_scr[s, r0:r0 + 2 * nc, n:2 * n] = jnp.concatenate(
                    [kept[0][1], kept[1][1]], axis=0).astype(BF16)
        return xr, xi

    def scans(s):
        zero = jnp.zeros((nc, n), F32)
        er, ei = scan(s, zero, zero, False)
        e_scr[s, :, 0:n] = er
        e_scr[s, :, n:2 * n] = ei
        llr = laml_ref[:, 0:n]
        lli = laml_ref[:, n:2 * n]
        cur_r = carry[s, :, 0:n]
        cur_i = carry[s, :, n:2 * n]
        for c in range(nc):
            xin_scr[s, c:c + 1, 0:n] = cur_r
            xin_scr[s, c:c + 1, n:2 * n] = cur_i
            e_r = e_scr[s, c:c + 1, 0:n]
            e_i = e_scr[s, c:c + 1, n:2 * n]
            cur_r, cur_i = llr * cur_r - lli * cur_i + e_r, llr * cur_i + lli * cur_r + e_i
        carry[s, :, 0:n] = cur_r
        carry[s, :, n:2 * n] = cur_i
        scan(s, xin_scr[s, :, 0:n], xin_scr[s, :, n:2 * n], True)

    def readout(s):
        step = 512
        for r0 in range(0, m_rows, step):
            y = jnp.dot(xs_scr[s, r0:r0 + step, :], cblk_ref[...], preferred_element_type=F32)
            y = y + d_ref[...] * u_rows(s, r0, r0 + step)
            y = _gelu_tanh(y)
            gate = (jnp.dot(y.astype(BF16), wglu_ref[...], preferred_element_type=F32)
                    + bglu_ref[...])
            out = y * jax.nn.sigmoid(gate)
            for hf in range(out.shape[1] // LANES):
                y_scr[s, hf, r0:r0 + step, :] = out[:, hf * LANES:(hf + 1) * LANES]
        for c in range(nc):
            for hf in range(y_scr.shape[1]):
                o_ref[s, c * lc:(c + 1) * lc, hf * LANES:(hf + 1) * LANES] = (
                    y_scr[s, hf, pl.ds(c, lc, stride=nc), :].astype(BF16))

    for s in range(pair):
        expand(s)
    for s in range(pair):
        scans(s)
        readout(s)


def _s5(u_a, u_b, lam, laml, bblk, cblk, d_row, wglu_blk, bglu_row):
    b, s, _ = u_a.shape
    w = 2 * LANES
    n = lam.shape[-1] // 2
    m_rows = S5_NCHUNK * S5_CHUNK
    pair = 2
    full = lambda a: pl.BlockSpec(a.shape, lambda i, j: (0,) * a.ndim)
    half = pl.BlockSpec((pair, m_rows, LANES), lambda i, j: (i, j, 0))
    return pl.pallas_call(
        functools.partial(_s5_kernel, n=n),
        grid=(b // pair, s // m_rows),
        in_specs=[half, half, full(lam), full(laml), full(bblk), full(cblk),
                  full(d_row), full(wglu_blk), full(bglu_row)],
        out_specs=pl.BlockSpec((pair, m_rows, w), lambda i, j: (i, j, 0)),
        out_shape=jax.ShapeDtypeStruct((b, s, w), BF16),
        scratch_shapes=[pltpu.VMEM((pair, m_rows, 2 * n), F32),
                        pltpu.VMEM((pair, m_rows, 2 * n), BF16),
                        pltpu.VMEM((pair, S5_NCHUNK, 2 * n), F32),
                        pltpu.VMEM((pair, S5_NCHUNK, 2 * n), F32),
                        pltpu.VMEM((pair, 1, 2 * n), F32),
                        pltpu.VMEM((pair, w // LANES, m_rows, LANES), F32)],
        compiler_params=_cparams(2),
        name="s5",
    )(u_a, u_b, lam, laml, bblk, cblk, d_row, wglu_blk, bglu_row)


def _foxprep_tile(fl, q_ref, k_ref, bf_ref, qa_ref, ka_ref, carry, cum_scr, heads, row0):
    rows = fl.shape[0]
    x = fl + bf_ref[...]
    lane = lax.broadcasted_iota(jnp.int32, (rows, LANES), 1)
    ls = jnp.minimum(x, 0.0) - jnp.log1p(jnp.exp(-jnp.abs(x)))
    ls = jnp.where(lane < heads, ls, 0.0)

    r = lax.broadcasted_iota(jnp.int32, (LANES, LANES), 0)
    c = lax.broadcasted_iota(jnp.int32, (LANES, LANES), 1)
    ltri = jnp.where(c <= r, 1.0, 0.0).astype(BF16)
    run = carry[...]
    for j in range(rows // LANES):
        blk = ls[j * LANES:(j + 1) * LANES]
        parts = jnp.dot(ltri, jnp.concatenate(_split3(blk), axis=1), preferred_element_type=F32)
        loc = parts[:, 0:LANES] + parts[:, LANES:2 * LANES] + parts[:, 2 * LANES:3 * LANES]
        cum_blk = loc + run
        run = cum_blk[LANES - 1:LANES, :]
        cum_scr[j * LANES:(j + 1) * LANES, :] = cum_blk
    carry[...] = run

    chi, cmid, clo = _split3(cum_scr[...] * LOG2E)
    packed = (chi.astype(F32) + pltpu.roll(cmid.astype(F32), heads, 1)
              + pltpu.roll(clo.astype(F32), 2 * heads, 1))
    lane1 = lax.broadcasted_iota(jnp.int32, (1, LANES), 1)
    for h in range(heads):
        pair = (h // 2) * LANES
        keep = (lane < HEAD_DIM) if h % 2 == 0 else (lane >= HEAD_DIM)
        base = HEAD_DIM if h % 2 == 0 else 0
        lanes_of = lambda first: sum(
            jnp.where(lane1 == base + (first + p) * heads + h, 1.0, 0.0) for p in range(3))
        part_lanes, ones_lanes = lanes_of(0), lanes_of(3)
        at_parts = packed if base == 0 else pltpu.roll(packed, base, 1)
        at_ones = pltpu.roll(packed, base + 3 * heads, 1)
        q_bias = (at_parts * part_lanes + ones_lanes).astype(BF16)
        k_bias = (part_lanes - at_ones * ones_lanes).astype(BF16)
        qa_ref[0, h, row0:row0 + rows, :] = jnp.where(keep, q_ref[:, pair:pair + LANES], q_bias)
        ka_ref[0, h, row0:row0 + rows, :] = jnp.where(keep, k_ref[:, pair:pair + LANES], k_bias)


def _premix_kernel(x_ref, mod_ref, w_ref, wf_ref, lng_ref, lnb_ref, ws_ref, bs_ref, bf_ref,
                   ygm_ref, s5a_ref, s5b_ref, v_ref, qa_ref, ka_ref,
                   h_scr, gu_scr, gv_scr, q_scr, k_scr, carry, cum_scr,
                   *, gm, s5w, fw, gm_heads, fox_heads):
    @pl.when(pl.program_id(1) == 0)
    def _():
        carry[...] = jnp.zeros_like(carry)

    n_sub, tm = h_scr.shape[0], h_scr.shape[1]
    assert n_sub * tm == S5_NCHUNK * S5_CHUNK
    sh = mod_ref[0, 0:1, :]
    sc = mod_ref[0, 1:2, :]
    fls = []
    for sub in range(n_sub):
        row0 = sub * tm
        h_scr[sub] = (x_ref[0, row0:row0 + tm, :] * (1.0 + sc) + sh).astype(BF16)
        h = h_scr[sub]

        def proj(lo, hi):
            return jnp.dot(h, w_ref[:, lo:hi], preferred_element_type=F32)

        o = 0
        gu_scr[sub] = proj(o, o + gm); o += gm
        gv_scr[sub] = proj(o, o + gm); o += gm
        u = proj(o, o + s5w); o += s5w
        chunk0 = row0 // S5_CHUNK
        for cl in range(tm // S5_CHUNK):
            for hf, ref in enumerate((s5a_ref, s5b_ref)):
                ref[0, pl.ds(chunk0 + cl, S5_CHUNK, stride=S5_NCHUNK), :] = (
                    u[cl * S5_CHUNK:(cl + 1) * S5_CHUNK, hf * LANES:(hf + 1) * LANES])
        q_scr[sub] = (proj(o, o + fw) * (LOG2E * HEAD_DIM ** -0.5)).astype(BF16); o += fw
        k_scr[sub] = proj(o, o + fw).astype(BF16); o += fw
        v_ref[0, row0:row0 + tm, :] = proj(o, o + fw).astype(BF16)
        fls.append(jnp.dot(h, wf_ref[...], preferred_element_type=F32))

    for sub in range(n_sub):
        row0 = sub * tm
        _gmlp_tile(gu_scr.at[sub], gv_scr.at[sub], lng_ref, lnb_ref, ws_ref, bs_ref, ygm_ref,
                   gm_heads, row0)
        _foxprep_tile(fls[sub], q_scr.at[sub], k_scr.at[sub], bf_ref, qa_ref, ka_ref, carry,
                      cum_scr.at[sub], fox_heads, row0)


def _premix(x, mod, w_in, ln_g_row, ln_b_row, w_s, b_s_full, bf_row, layer, gm, s5w, fw):
    b, s, d = x.shape
    tm = ROW_TILE
    blk = S5_NCHUNK * S5_CHUNK
    n_sub = blk // tm
    n_main = 2 * gm + s5w + 3 * fw
    w_main_spec = _layer_spec(w_in, layer, (d, n_main))
    w_f_spec = pl.BlockSpec((None, d, LANES), lambda *_: (layer, 0, n_main // LANES))
    gm_heads = w_s.shape[1]
    fox_heads = fw // HEAD_DIM
    row = lambda n: pl.BlockSpec((1, blk, n), lambda i, j: (i, j, 0))
    full = lambda a: pl.BlockSpec(a.shape, lambda i, j: (0,) * a.ndim)
    aug = pl.BlockSpec((1, fox_heads, blk, LANES), lambda i, j: (i, 0, j, 0))
    lay = lambda a: _layer_spec(a, layer)
    assert s5w == 2 * LANES
    s5_half = row(LANES)
    outs = [jax.ShapeDtypeStruct((b, s, gm), BF16),
            jax.ShapeDtypeStruct((b, s, LANES), F32), jax.ShapeDtypeStruct((b, s, LANES), F32),
            jax.ShapeDtypeStruct((b, s, fw), BF16),
            jax.ShapeDtypeStruct((b, fox_heads, s, LANES), BF16),
            jax.ShapeDtypeStruct((b, fox_heads, s, LANES), BF16)]
    return pl.pallas_call(
        functools.partial(_premix_kernel, gm=gm, s5w=s5w, fw=fw, gm_heads=gm_heads,
                          fox_heads=fox_heads),
        grid=(b, s // blk),
        in_specs=[row(d), _mod_spec(mod, layer), w_main_spec, w_f_spec, lay(ln_g_row),
                  lay(ln_b_row), lay(w_s), lay(b_s_full), lay(bf_row)],
        out_specs=[row(gm), s5_half, s5_half, row(fw), aug, aug],
        out_shape=outs,
        scratch_shapes=[pltpu.VMEM((n_sub, tm, d), BF16),
                        pltpu.VMEM((n_sub, tm, gm), F32), pltpu.VMEM((n_sub, tm, gm), F32),
                        pltpu.VMEM((n_sub, tm, fw), BF16), pltpu.VMEM((n_sub, tm, fw), BF16),
                        pltpu.VMEM((1, LANES), F32), pltpu.VMEM((n_sub, tm, LANES), F32)],
        compiler_params=_cparams(2),
        name="premix",
    )(x, mod, w_in, w_in, ln_g_row, ln_b_row, w_s, b_s_full, bf_row)


def _fox_kernel(qa_ref, ka_ref, v_ref, o_ref, va_scr):
    nh = qa_ref.shape[1]
    s_len = qa_ref.shape[2]
    tq = FOX_BLOCK
    tk = tq
    nt = (((1,), (1,)), ((), ()))

    lane_v = lax.broadcasted_iota(jnp.int32, va_scr.shape[1:], 1)
    one = jnp.ones((), BF16)
    for hp in range(nh // 2):
        v2 = v_ref[0, :, hp * LANES:(hp + 1) * LANES]
        va_scr[2 * hp] = jnp.where(lane_v < HEAD_DIM, v2, one)
        va_scr[2 * hp + 1] = jnp.where(lane_v >= HEAD_DIM, v2, one)

    def step(q0, k0, width, state, mask_off):
        logits = [lax.dot_general(qa_ref[0, hh, q0:q0 + tq, :], ka_ref[0, hh, k0:k0 + width, :],
                                  nt, preferred_element_type=F32) for hh in range(nh)]
        new = []
        for hh in range(nh):
            m, acc = state[hh]
            s = logits[hh]
            if mask_off is not None:
                row = lax.broadcasted_iota(jnp.int32, (tq, width), 0)
                col = lax.broadcasted_iota(jnp.int32, (tq, width), 1)
                s = jnp.where(col <= row + mask_off, s, NEG_INF)
            m_new = jnp.maximum(m, jnp.max(s, axis=-1, keepdims=True))
            alpha = jnp.exp2(m - m_new)
            p = jnp.exp2(s - m_new)
            pv = jnp.dot(p.astype(BF16), va_scr[hh, k0:k0 + width, :],
                         preferred_element_type=F32)
            new.append((m_new, alpha * acc + pv))
        return tuple(new)

    wide = 2 * tk
    lane = lax.broadcasted_iota(jnp.int32, (tq, LANES), 1)
    for qi in range(s_len // tq):
        q0 = qi * tq
        state = tuple((jnp.full((tq, 1), -jnp.inf, F32), jnp.zeros((tq, LANES), F32))
                      for _ in range(nh))
        for j in range(qi // 2):
            state = step(q0, j * wide, wide, state, None)
        k_tail = (qi // 2) * wide
        if qi % 2 == 1:
            state = step(q0, k_tail, wide, state, tk)
        else:
            state = step(q0, k_tail, tk, state, 0)
        outs = [acc / pltpu.roll(acc, HEAD_DIM, 1) for _, acc in state]
        for hp in range(nh // 2):
            o_ref[0, q0:q0 + tq, hp * LANES:(hp + 1) * LANES] = jnp.where(
                lane < HEAD_DIM, outs[2 * hp], outs[2 * hp + 1]).astype(BF16)


def _fox(q_aug, k_aug, v):
    b, heads, s, _ = q_aug.shape
    fw = v.shape[-1]
    nh = FOX_HEADS_PER_STEP
    vw = nh // 2 * LANES
    seq = lambda: pl.BlockSpec((1, nh, s, LANES), lambda i, p: (i, p, 0, 0))
    return pl.pallas_call(
        _fox_kernel,
        grid=(b, heads // nh),
        in_specs=[seq(), seq(), pl.BlockSpec((1, s, vw), lambda i, p: (i, 0, p))],
        out_specs=pl.BlockSpec((1, s, vw), lambda i, p: (i, 0, p)),
        out_shape=jax.ShapeDtypeStruct((b, s, fw), BF16),
        scratch_shapes=[pltpu.VMEM((nh, s, LANES), BF16)],
        compiler_params=_cparams(2),
        name="fox",
    )(q_aug, k_aug, v)


def _mixffn_kernel(ygm_ref, ys5_ref, yfox_ref, wout_ref, x_ref, mod_ref, g1_ref, b1_ref,
                   wup_ref, cw_ref, cb_ref, wdn_ref, g2_ref, b2_ref, o_ref,
                   x1_scr, h_scr, halo, act_scr, *, alpha, gm, s5w, dff, chunk):
    @pl.when(pl.program_id(1) == 0)
    def _():
        halo[...] = jnp.zeros_like(halo)

    y_cat = jnp.concatenate([ygm_ref[0], ys5_ref[0], yfox_ref[0]], axis=1)
    mix = jnp.dot(y_cat, wout_ref[...], preferred_element_type=F32)
    gate1 = mod_ref[0, 2:3, :]
    x1_scr[...] = _layer_norm_rows(alpha * x_ref[0] + (1.0 + gate1) * mix, g1_ref[...], b1_ref[...])

    rows = x_ref.shape[1]
    sh = mod_ref[0, 3:4, :]
    sc = mod_ref[0, 4:5, :]
    gate = mod_ref[0, 5:6, :]
    h_scr[...] = (x1_scr[...] * (1.0 + sc) + sh).astype(BF16)
    h = h_scr[...]
    rid = lax.broadcasted_iota(jnp.int32, (rows, chunk), 0)
    for ci in range(dff // chunk):
        c0 = ci * chunk
        a = jnp.dot(h, wup_ref[:, c0:c0 + chunk], preferred_element_type=F32)
        gt = jnp.dot(h, wup_ref[:, dff + c0:dff + c0 + chunk], preferred_element_type=F32)
        p1 = halo[SUBLANES - 1:SUBLANES, c0:c0 + chunk]
        p2 = halo[SUBLANES - 2:SUBLANES - 1, c0:c0 + chunk]
        a1 = jnp.where(rid == 0, p1, pltpu.roll(a, 1, 0))
        a2 = jnp.where(rid == 0, p2, jnp.where(rid == 1, p1, pltpu.roll(a, 2, 0)))
        halo[:, c0:c0 + chunk] = a[rows - SUBLANES:rows, :]
        conv = (cb_ref[:, c0:c0 + chunk] + cw_ref[0:1, c0:c0 + chunk] * a2
                + cw_ref[1:2, c0:c0 + chunk] * a1 + cw_ref[2:3, c0:c0 + chunk] * a)
        act_scr[:, c0:c0 + chunk] = (_gelu_tanh(conv) * gt).astype(BF16)
    ffn = jnp.dot(act_scr[...], wdn_ref[...], preferred_element_type=F32)
    y = alpha * x1_scr[...] + (1.0 + gate) * ffn
    o_ref[0] = _layer_norm_rows(y, g2_ref[...], b2_ref[...])


def _mixffn(y_gm, y_s5, y_fox, w_out, x, mod, g1_row, b1_row,
            w_up, conv_w, conv_b_row, w_down, g2_row, b2_row, layer, alpha):
    b, s, d = x.shape
    gm, s5w, fw = y_gm.shape[-1], y_s5.shape[-1], y_fox.shape[-1]
    dff = w_down.shape[1]
    tm = FFN_ROW_TILE
    chunk = 256
    row = lambda n: pl.BlockSpec((1, tm, n), lambda i, j: (i, j, 0))
    full = lambda a: _layer_spec(a, layer)
    resident = lambda a: pl.BlockSpec((None,) + a.shape[1:], lambda i, j: (layer,) + (0,) * (a.ndim - 1),
                                      pipeline_mode=pl.Buffered(1))
    return pl.pallas_call(
        functools.partial(_mixffn_kernel, alpha=alpha, gm=gm, s5w=s5w, dff=dff, chunk=chunk),
        grid=(b, s // tm),
        in_specs=[row(gm), row(s5w), row(fw), resident(w_out), row(d),
                  _mod_spec(mod, layer), full(g1_row), full(b1_row),
                  resident(w_up), full(conv_w), full(conv_b_row), resident(w_down),
                  full(g2_row), full(b2_row)],
        out_specs=row(d),
        out_shape=jax.ShapeDtypeStruct((b, s, d), F32),
        scratch_shapes=[pltpu.VMEM((tm, d), F32),
                        pltpu.VMEM((tm, d), BF16),
                        pltpu.VMEM((SUBLANES, dff), F32),
                        pltpu.VMEM((tm, dff), BF16)],
        compiler_params=_cparams(2),
        name="mixffn",
    )(y_gm, y_s5, y_fox, w_out, x, mod, g1_row, b1_row,
      w_up, conv_w, conv_b_row, w_down, g2_row, b2_row)


def _block_diag(a):
    depth, g, r, c = a.shape
    eye = jnp.eye(g, dtype=a.dtype)
    return (a[:, :, :, None, :] * eye[None, :, None, :, None]).reshape(depth, g * r, g * c)


def kernel(x, c, w_ada, b_ada, w_in, b_f, gm_ln_g, gm_ln_b, gm_w_s, gm_b_s, s5_lam_re, s5_lam_im, s5_log_dt, s5_b_re, s5_b_im, s5_c_re, s5_c_im, s5_d, s5_w_glu, s5_b_glu, w_out, ln1_g, ln1_b, w_up, conv_w, conv_b, w_down, ln2_g, ln2_b):
    depth, d, _ = w_in.shape
    batch = x.shape[0]
    gm_heads = gm_w_s.shape[1]
    gm = gm_heads * HEAD_DIM
    groups = s5_lam_re.shape[1]
    s5w = groups * S5_GROUP_DIM
    fox_heads = b_f.shape[1]
    fw = fox_heads * HEAD_DIM
    n_main = 2 * gm + s5w + 3 * fw
    alpha = (2.0 * depth) ** 0.25

    pad_rows = 2 * SUBLANES
    c_pad = jnp.pad(c, ((0, pad_rows - batch), (0, 0)))
    mod = _adaln(c_pad, w_ada, b_ada[:, None, :])[:, :batch]
    mod = mod.reshape(depth, batch, 6, d)

    n_state = groups * S5_STATE
    lre = s5_lam_re.reshape(depth, 1, n_state)
    lim = s5_lam_im.reshape(depth, 1, n_state)
    ldt = jnp.repeat(s5_log_dt, S5_STATE, axis=-1).reshape(depth, 1, n_state)
    bre_blk = _block_diag(jnp.swapaxes(s5_b_re, -1, -2))
    bim_blk = _block_diag(jnp.swapaxes(s5_b_im, -1, -2))
    cre_blk = _block_diag(jnp.swapaxes(s5_c_re, -1, -2))
    cim_blk = _block_diag(jnp.swapaxes(s5_c_im, -1, -2))
    lam, laml, bblk, cblk = _s5prep(lre, lim, ldt, bre_blk, bim_blk, cre_blk, cim_blk)
    wglu_blk = _block_diag(jnp.swapaxes(s5_w_glu, -1, -2)).astype(BF16)

    w_in_bf = jnp.pad(w_in, ((0, 0), (0, 0), (0, LANES - fox_heads))).astype(BF16)
    w_out_bf = w_out.astype(BF16)
    w_up_bf = w_up.astype(BF16)
    w_down_bf = w_down.astype(BF16)
    row3 = lambda a: a.reshape(depth, 1, -1)
    gm_g, gm_b = row3(gm_ln_g), row3(gm_ln_b)
    bs_full = jnp.repeat(jnp.swapaxes(gm_b_s, 1, 2), HEAD_DIM, axis=2)
    s5_d_row, s5_bglu_row = row3(s5_d), row3(s5_b_glu)
    bf_row = jnp.pad(b_f[:, None, :], ((0, 0), (0, 0), (0, LANES - fox_heads)))
    ln1_g3, ln1_b3, ln2_g3, ln2_b3 = row3(ln1_g), row3(ln1_b), row3(ln2_g), row3(ln2_b)
    conv_b3 = row3(conv_b)

    for l in range(depth):
        y_gm, s5_a, s5_b, v, q_aug, k_aug = _premix(x, mod, w_in_bf, gm_g, gm_b, gm_w_s, bs_full,
                                                    bf_row, l, gm, s5w, fw)
        y_s5 = _s5(s5_a, s5_b, lam[l], laml[l], bblk[l], cblk[l], s5_d_row[l], wglu_blk[l],
                   s5_bglu_row[l])
        y_fox = _fox(q_aug, k_aug, v)
        x = _mixffn(y_gm, y_s5, y_fox, w_out_bf, x, mod, ln1_g3, ln1_b3,
                    w_up_bf, conv_w, conv_b3, w_down_bf, ln2_g3, ln2_b3, l, alpha)
    return x
```
